```python
import math
import jax, jax.numpy as jnp
from jax import lax
import numpy as np

D_MODEL = 1024
BATCH = 8
SEQ = 4096
DEPTH = 4

GRID_W = 64
CTX_LEN = 256
N_MIXERS = 2
N_NA_LAYERS = (DEPTH + N_MIXERS - 1) // N_MIXERS
N_HY_LAYERS = DEPTH // N_MIXERS
N_HEADS = 16
HEAD_DIM = D_MODEL // N_HEADS
KH_MAX = 8
KW = 16
Q_BLOCK_W = 16
K_BLOCK_W = Q_BLOCK_W + KW
N_COL_BLOCKS = GRID_W // Q_BLOCK_W
HY_ORDER = 2
HY_EMB_DIM = 33
HY_FILTER_ORDER = 64
HY_N_SIN = 3
HY_FAST_DECAY = 0.3
HY_SLOW_DECAY = 1.5
HY_TARGET = 1e-2
LN_EPS = 1e-5
ALPHA = (2 * DEPTH) ** 0.25
BETA = (8 * DEPTH) ** -0.25
NEG_INF = -1e30

kernel_name = "hybrid_na_hyena_deepnorm_prefix"


def layer_norm(x, g, b):
    xf = x.astype(jnp.float32)
    mu = xf.mean(-1, keepdims=True)
    var = jnp.square(xf - mu).mean(-1, keepdims=True)
    return ((xf - mu) * lax.rsqrt(var + LN_EPS)).astype(x.dtype) * g + b


def _col_block_layout():
    q_cols = np.arange(GRID_W).reshape(N_COL_BLOCKS, Q_BLOCK_W)
    q_start = np.clip(q_cols - KW // 2, 0, GRID_W - KW)
    blk_start = np.clip(np.arange(N_COL_BLOCKS) * Q_BLOCK_W - KW // 2, 0, GRID_W - K_BLOCK_W)
    k_cols = blk_start[:, None] + np.arange(K_BLOCK_W)
    kc = k_cols[:, None, :]
    in_win = (kc >= q_start[:, :, None]) & (kc < q_start[:, :, None] + KW)
    dcol = np.clip(kc - q_cols[:, :, None] + KW - 1, 0, 2 * KW - 2)
    return k_cols, in_win, dcol


def neighbourhood_attention(q, k, v, k_ctx, v_ctx, rpb):
    B, S, H, Dh = q.shape
    rows = S // GRID_W
    kh = min(KH_MAX, rows)
    k_cols, in_win, dcol = _col_block_layout()
    k_grid = k.reshape(B, rows, GRID_W, H, Dh)
    v_grid = v.reshape(B, rows, GRID_W, H, Dh)
    q_rows = jnp.moveaxis(q.reshape(B, rows, N_COL_BLOCKS, Q_BLOCK_W, H, Dh), 1, 0) * (Dh ** -0.5)
    mask = jnp.asarray(in_win)[None, None, :, :, None, :]
    n_lat = kh * K_BLOCK_W

    def one_row(args):
        r, q_r = args
        r0 = jnp.clip(r - kh // 2, 0, rows - kh)
        k_r = lax.dynamic_slice_in_dim(k_grid, r0, kh, axis=1)[:, :, k_cols]
        v_r = lax.dynamic_slice_in_dim(v_grid, r0, kh, axis=1)[:, :, k_cols]
        drow = r0 + jnp.arange(kh) - r + KH_MAX - 1
        bias = jnp.take(rpb, drow, axis=1)[:, :, dcol]
        bias = jnp.transpose(bias, (0, 2, 3, 1, 4)).astype(jnp.float32)
        s_lat = jnp.einsum('bnqhd,binjhd->bhnqij', q_r, k_r).astype(jnp.float32) + bias[None]
        s_lat = jnp.where(mask, s_lat, NEG_INF).reshape(B, H, N_COL_BLOCKS, Q_BLOCK_W, n_lat)
        s_ctx = jnp.einsum('bnqhd,bchd->bhnqc', q_r, k_ctx).astype(jnp.float32)
        p = jax.nn.softmax(jnp.concatenate([s_lat, s_ctx], axis=-1), axis=-1).astype(v.dtype)
        p_lat = p[..., :n_lat].reshape(B, H, N_COL_BLOCKS, Q_BLOCK_W, kh, K_BLOCK_W)
        p_ctx = p[..., n_lat:]
        o = (jnp.einsum('bhnqij,binjhd->bnqhd', p_lat, v_r)
             + jnp.einsum('bhnqc,bchd->bnqhd', p_ctx, v_ctx))
        return o.reshape(B, GRID_W, H, Dh)

    out = lax.map(one_row, (jnp.arange(rows), q_rows))
    return jnp.moveaxis(out, 0, 1).reshape(B, S, H, Dh)


def dense_attention(q, k, v):
    s = jnp.einsum('bqhd,bkhd->bhqk', q * (q.shape[-1] ** -0.5), k).astype(jnp.float32)
    p = jax.nn.softmax(s, axis=-1).astype(v.dtype)
    return jnp.einsum('bhqk,bkhd->bqhd', p, v)


def _heads(t):
    return t.reshape(t.shape[0], t.shape[1], N_HEADS, HEAD_DIM)


def na_mixer(h, hc, w_in, w_out, rpb, with_ctx_out):
    B, S, D = h.shape
    q, k, v, z = jnp.split(h @ w_in, 4, axis=-1)
    if with_ctx_out:
        qc, kc, vc, zc = jnp.split(hc @ w_in, 4, axis=-1)
    else:
        kc, vc = jnp.split(hc @ w_in[:, D:3 * D], 2, axis=-1)
    o = neighbourhood_attention(_heads(q), _heads(k), _heads(v), _heads(kc), _heads(vc), rpb).reshape(B, S, D)
    y = (o * jax.nn.silu(z)) @ w_out
    if not with_ctx_out:
        return y, None
    oc = dense_attention(_heads(qc), _heads(kc), _heads(vc)).reshape(hc.shape)
    yc = (oc * jax.nn.silu(zc)) @ w_out
    return y, yc


def short_conv(u, w, b):
    up = jnp.pad(u, ((0, 0), (1, 1), (0, 0)))
    return up[:, :-2] * w[0] + up[:, 1:-1] * w[1] + up[:, 2:] * w[2] + b


def hyena_filters(L, w1, b1, w2, b2, w3, b3, w4, freq):
    t = jnp.linspace(0.0, 1.0, L, dtype=jnp.float32)[:, None]
    bands = (HY_EMB_DIM - 1) // 2
    wpos = 2.0 * math.pi * jnp.arange(L, dtype=jnp.float32)[:, None] / L
    f = jnp.linspace(1e-4, bands - 1, bands, dtype=jnp.float32)[None, :]
    z = jnp.concatenate([t, jnp.cos(f * wpos), -jnp.sin(f * wpos)], axis=-1)
    a = jnp.sin(freq[0] * (z @ w1 + b1))
    a = jnp.sin(freq[1] * (a @ w2 + b2))
    a = jnp.sin(freq[2] * (a @ w3 + b3))
    width = w4.shape[-1] // (2 * HY_ORDER)
    hf = (a @ w4).astype(jnp.float32).reshape(L, HY_ORDER, 2, width)
    max_decay = math.log(HY_TARGET) / HY_FAST_DECAY
    min_decay = math.log(HY_TARGET) / HY_SLOW_DECAY
    deltas = jnp.abs(jnp.linspace(min_decay, max_decay, width, dtype=jnp.float32))
    hf = hf * jnp.exp(-t[:, :, None, None] * deltas)
    fwd = hf[:, :, 0]
    bwd = hf[1:, :, 1][::-1]
    taps = jnp.concatenate([fwd, jnp.zeros((1,) + fwd.shape[1:], jnp.float32), bwd], axis=0)
    taps = taps / jnp.sum(jnp.abs(taps), axis=0, keepdims=True)
    return jnp.fft.rfft(taps, n=2 * L, axis=0)


def long_conv(u, h_fft, skip):
    L = u.shape[1]
    uf = jnp.fft.rfft(u.astype(jnp.float32), n=2 * L, axis=1)
    y = jnp.fft.irfft(uf * h_fft[None], n=2 * L, axis=1)[:, :L]
    return (y + u.astype(jnp.float32) * skip).astype(u.dtype)


def hyena_mixer(h, w_in, w_out, conv_w, conv_b, filt, skip):
    L = h.shape[1]
    proj = h @ w_in
    width = proj.shape[-1] // 4
    pre, g = proj[..., :3 * width], proj[..., 3 * width:]
    v, x1, x2 = jnp.split(short_conv(pre, conv_w, conv_b), 3, axis=-1)
    h_fft = hyena_filters(L, *filt)
    z = x1 * long_conv(v, h_fft[:, 0], skip[0])
    z = x2 * long_conv(z, h_fft[:, 1], skip[1])
    return (z * jax.nn.silu(g)) @ w_out


def setup_inputs(seed: int = 0) -> dict:
    key = jax.random.key(seed)
    ks = jax.random.split(key, 22)
    D = D_MODEL
    f32 = jnp.float32

    def nrm(k, shape, s):
        return jax.random.normal(k, shape, f32) * s

    return {
        "x": nrm(ks[0], (BATCH, SEQ, D), 1.0),
        "c": nrm(ks[1], (BATCH, D), 1.0),
        "ctx": nrm(ks[2], (BATCH, CTX_LEN, D), 1.0),
        "c_ctx": nrm(ks[3], (D,), 1.0),
        "w_ada": nrm(ks[4], (DEPTH, D, 3 * D), 0.5 * D ** -0.5),
        "b_ada": nrm(ks[5], (DEPTH, 3 * D), 0.02),
        "w_in": nrm(ks[6], (DEPTH, D, 4 * D), D ** -0.5),
        "w_out": nrm(ks[7], (DEPTH, D, D), BETA * D ** -0.5),
        "ln_g": 1.0 + nrm(ks[8], (DEPTH, D), 0.05),
        "ln_b": nrm(ks[9], (DEPTH, D), 0.02),
        "na_rpb": nrm(ks[10], (N_NA_LAYERS, N_HEADS, 2 * KH_MAX - 1, 2 * KW - 1), 0.1),
        "hy_conv_w": nrm(ks[11], (N_HY_LAYERS, 3, 3 * D), 3 ** -0.5),
        "hy_conv_b": nrm(ks[12], (N_HY_LAYERS, 3 * D), 0.02),
        "hy_f_w1": nrm(ks[13], (N_HY_LAYERS, HY_EMB_DIM, HY_FILTER_ORDER), HY_EMB_DIM ** -0.5),
        "hy_f_b1": nrm(ks[14], (N_HY_LAYERS, HY_FILTER_ORDER), 0.1),
        "hy_f_w2": nrm(ks[15], (N_HY_LAYERS, HY_FILTER_ORDER, HY_FILTER_ORDER), HY_FILTER_ORDER ** -0.5),
        "hy_f_b2": nrm(ks[16], (N_HY_LAYERS, HY_FILTER_ORDER), 0.1),
        "hy_f_w3": nrm(ks[17], (N_HY_LAYERS, HY_FILTER_ORDER, HY_FILTER_ORDER), HY_FILTER_ORDER ** -0.5),
        "hy_f_b3": nrm(ks[18], (N_HY_LAYERS, HY_FILTER_ORDER), 0.1),
        "hy_f_w4": nrm(ks[19], (N_HY_LAYERS, HY_FILTER_ORDER, 2 * HY_ORDER * D), HY_FILTER_ORDER ** -0.5),
        "hy_f_freq": 1.0 + nrm(ks[20], (N_HY_LAYERS, HY_N_SIN, HY_FILTER_ORDER), 0.05),
        "hy_skip": nrm(ks[21], (N_HY_LAYERS, HY_ORDER, D), 0.5),
    }


def reference(x, c, ctx, c_ctx, w_ada, b_ada, w_in, w_out, ln_g, ln_b, na_rpb, hy_conv_w, hy_conv_b,
              hy_f_w1, hy_f_b1, hy_f_w2, hy_f_b2, hy_f_w3, hy_f_b3, hy_f_w4, hy_f_freq, hy_skip):
    sc = jax.nn.silu(c)
    sc_ctx = jax.nn.silu(c_ctx)
    for i in range(DEPTH):
        last = i == DEPTH - 1
        j = i // N_MIXERS
        shift, scale, gate = jnp.split((sc @ w_ada[i] + b_ada[i])[:, None, :], 3, axis=-1)
        shift_c, scale_c, gate_c = jnp.split(sc_ctx @ w_ada[i] + b_ada[i], 3, axis=-1)
        h = x * (1 + scale) + shift
        hc = ctx * (1 + scale_c) + shift_c
        if i % N_MIXERS == 0:
            y, yc = na_mixer(h, hc, w_in[i], w_out[i], na_rpb[j], not last)
        else:
            filt = (hy_f_w1[j], hy_f_b1[j], hy_f_w2[j], hy_f_b2[j], hy_f_w3[j], hy_f_b3[j],
                    hy_f_w4[j], hy_f_freq[j])
            y = hyena_mixer(h, w_in[i], w_out[i], hy_conv_w[j], hy_conv_b[j], filt, hy_skip[j])
            yc = None if last else hyena_mixer(hc, w_in[i], w_out[i], hy_conv_w[j], hy_conv_b[j], filt, hy_skip[j])
        x = layer_norm(ALPHA * x + gate * y, ln_g[i], ln_b[i])
        if not last:
            ctx = layer_norm(ALPHA * ctx + gate_c * yc, ln_g[i], ln_b[i])
    return x
```

```python
import functools
import math

import numpy as np
import jax
import jax.numpy as jnp
from jax import lax
from jax.experimental import pallas as pl
from jax.experimental.pallas import tpu as pltpu

F32 = jnp.float32
BF16 = jnp.bfloat16

HEAD_DIM = 64
GRID_W = 64
KH = 8
KW = 16
Q_BLOCK_W = 16
K_BLOCK_W = Q_BLOCK_W + KW
N_COL_BLOCKS = GRID_W // Q_BLOCK_W
RPB_ROWS = 2 * KH - 1
RPB_COLS = 2 * KW - 1
HY_EMB_DIM = 33
HY_FAST_DECAY = 0.3
HY_SLOW_DECAY = 1.5
HY_TARGET = 1e-2
LN_EPS = 1e-5
NEG_INF = -1e30

LANES = 128
SUBLANES = 8
HEADS_PER_TILE = LANES // HEAD_DIM
VMEM_LIMIT = 56 * 1024 * 1024

Q_ROWS = 8
K_ROWS = 16
ROWS_PER_LANE_TILE = LANES // K_BLOCK_W
N_DSTART = 28
DSTART_SHIFT = 8
CONV_BLOCK = 512
FREQ_PAD = 8
MAX_ACC_ROWS = 40


def _cparams(sem):
    return pltpu.CompilerParams(dimension_semantics=sem, vmem_limit_bytes=VMEM_LIMIT)


def _split_bf16(a):
    hi = a.astype(BF16)
    lo = (a - hi.astype(F32)).astype(BF16)
    return hi, lo


def _dot(a, b):
    return jnp.dot(a, b, preferred_element_type=F32)


def _dot_nt(a, b):
    return lax.dot_general(a, b, (((1,), (1,)), ((), ())), preferred_element_type=F32)


def _dot3(a, b):
    ah, al = _split_bf16(a)
    bh, bl = _split_bf16(b)
    return _dot(ah, bh) + _dot(al, bh) + _dot(ah, bl)


def _silu(x):
    return x * jax.nn.sigmoid(x)


def _ada_kernel(cc_ref, w_ref, b_ref, o_ref):
    o_ref[0] = _dot3(_silu(cc_ref[...]), w_ref[0]) + b_ref[0]


def _ada_vectors(cc, w_ada, b_ada):
    depth, d, n = w_ada.shape
    tn = min(n, 1024)
    return pl.pallas_call(
        _ada_kernel,
        grid=(depth, n // tn),
        in_specs=[
            pl.BlockSpec(cc.shape, lambda i, j: (0, 0)),
            pl.BlockSpec((1, d, tn), lambda i, j: (i, 0, j)),
            pl.BlockSpec((1, 1, tn), lambda i, j: (i, 0, j)),
        ],
        out_specs=pl.BlockSpec((1, cc.shape[0], tn), lambda i, j: (i, 0, j)),
        out_shape=jax.ShapeDtypeStruct((depth, cc.shape[0], n), F32),
        compiler_params=_cparams(("arbitrary", "arbitrary")),
        name="ada_vectors",
    )(cc, w_ada, b_ada.reshape(depth, 1, n))


def _inproj_kernel(x_ref, sc_ref, sh_ref, w_ref, o_ref, h_scr):
    @pl.when(pl.program_id(2) == 0)
    def _():
        h_scr[...] = (x_ref[0] * (1.0 + sc_ref[0]) + sh_ref[0]).astype(BF16)

    o_ref[0] = _dot(h_scr[...], w_ref[...])


def _in_projection(x, scale, shift, w_bf16):
    b, s, d = x.shape
    n = w_bf16.shape[1]
    tm = min(s, 1024)
    tn = min(n, 1024)
    return pl.pallas_call(
        _inproj_kernel,
        grid=(b, s // tm, n // tn),
        in_specs=[
            pl.BlockSpec((1, tm, d), lambda bi, i, j: (bi, i, 0)),
            pl.BlockSpec((1, 1, d), lambda bi, i, j: (bi, 0, 0)),
            pl.BlockSpec((1, 1, d), lambda bi, i, j: (bi, 0, 0)),
            pl.BlockSpec((d, tn), lambda bi, i, j: (0, j)),
        ],
        out_specs=pl.BlockSpec((1, tm, tn), lambda bi, i, j: (bi, i, j)),
        out_shape=jax.ShapeDtypeStruct((b, s, n), F32),
        scratch_shapes=[pltpu.VMEM((tm, d), BF16)],
        compiler_params=_cparams(("arbitrary", "arbitrary", "arbitrary")),
        name="in_projection",
    )(x, scale, shift, w_bf16)


def _outproj_kernel(alpha, a_ref, z_ref, x_ref, gate_ref, w_ref, g_ref, b_ref, o_ref):
    a = a_ref[0] * _silu(z_ref[0])
    y = _dot(a.astype(BF16), w_ref[...])
    r = alpha * x_ref[0] + gate_ref[0] * y
    mu = jnp.mean(r, axis=-1, keepdims=True)
    dlt = r - mu
    var = jnp.mean(dlt * dlt, axis=-1, keepdims=True)
    o_ref[0] = dlt * lax.rsqrt(var + LN_EPS) * g_ref[...] + b_ref[...]


def _out_projection(a, proj, x, gate, w_bf16, ln_g, ln_b, alpha):
    b, s, d = x.shape
    tm = min(s, 512)
    row = lambda bi, i: (bi, i, 0)
    return pl.pallas_call(
        functools.partial(_outproj_kernel, alpha),
        grid=(b, s // tm),
        in_specs=[
            pl.BlockSpec((1, tm, d), row),
            pl.BlockSpec((1, tm, d), lambda bi, i: (bi, i, 3)),
            pl.BlockSpec((1, tm, d), row),
            pl.BlockSpec((1, 1, d), lambda bi, i: (bi, 0, 0)),
            pl.BlockSpec((d, d), lambda bi, i: (0, 0)),
            pl.BlockSpec((1, d), lambda bi, i: (0, 0)),
            pl.BlockSpec((1, d), lambda bi, i: (0, 0)),
        ],
        out_specs=pl.BlockSpec((1, tm, d), row),
        out_shape=jax.ShapeDtypeStruct((b, s, d), F32),
        compiler_params=_cparams(("arbitrary", "arbitrary")),
        name="out_projection",
    )(a, proj, x, gate, w_bf16, ln_g.reshape(1, d), ln_b.reshape(1, d))


def _col_layout():
    q_cols = np.arange(GRID_W).reshape(N_COL_BLOCKS, Q_BLOCK_W)
    q_start = np.clip(q_cols - KW // 2, 0, GRID_W - KW)
    blk_start = np.clip(np.arange(N_COL_BLOCKS) * Q_BLOCK_W - KW // 2, 0, GRID_W - K_BLOCK_W)
    k_cols = blk_start[:, None] + np.arange(K_BLOCK_W)
    kc = k_cols[:, None, :]
    in_win = (kc >= q_start[:, :, None]) & (kc < q_start[:, :, None] + KW)
    dcol = np.clip(kc - q_cols[:, :, None] + KW - 1, 0, RPB_COLS - 1)
    return blk_start, in_win, dcol


def _window_base(rb, rows):
    return np.clip(rb * Q_ROWS - KH // 2, 0, rows - K_ROWS)


def _na_mask_table(rows):
    _, in_win, _ = _col_layout()
    n_rb = rows // Q_ROWS
    out = np.zeros((3, N_COL_BLOCKS, Q_ROWS, K_ROWS // ROWS_PER_LANE_TILE, Q_BLOCK_W, LANES), np.float32)
    for cls, rb in enumerate((0, 1, n_rb - 1)):
        kr0 = _window_base(rb, rows)
        for qr in range(Q_ROWS):
            r = rb * Q_ROWS + qr
            r0 = np.clip(r - KH // 2, 0, rows - KH)
            for kr in range(K_ROWS):
                row_ok = r0 <= kr0 + kr < r0 + KH
                g, j = divmod(kr, ROWS_PER_LANE_TILE)
                ok = in_win & row_ok
                out[cls, :, qr, g, :, j * K_BLOCK_W:(j + 1) * K_BLOCK_W] = np.where(ok, 0.0, NEG_INF)
    return out


def _rpb_expand_kernel(r_ref, e_ref, o_ref):
    r = r_ref[...]
    hi = r.astype(BF16)
    r1 = r - hi.astype(F32)
    mid = r1.astype(BF16)
    lo = (r1 - mid.astype(F32)).astype(BF16)
    e = e_ref[...]
    o_ref[...] = _dot(hi, e) + _dot(mid, e) + _dot(lo, e)


def _na_bias_table(rpb):
    h = rpb.shape[0]
    _, _, dcol = _col_layout()
    kpad = 32
    onehot = (dcol.reshape(-1)[None, :] == np.arange(kpad)[:, None]).astype(np.float32)
    r2 = jnp.pad(rpb.reshape(h * RPB_ROWS, RPB_COLS), ((0, 0), (0, kpad - RPB_COLS)))
    mrows = -(-r2.shape[0] // 16) * 16
    r2 = jnp.pad(r2, ((0, mrows - r2.shape[0]), (0, 0)))
    g = pl.pallas_call(
        _rpb_expand_kernel,
        out_shape=jax.ShapeDtypeStruct((mrows, onehot.shape[1]), F32),
        name="rpb_expand",
    )(r2, jnp.asarray(onehot, BF16))
    g = g[:h * RPB_ROWS].reshape(h, RPB_ROWS, N_COL_BLOCKS, Q_BLOCK_W, K_BLOCK_W)
    hi_pad = N_DSTART + ROWS_PER_LANE_TILE - 1 - DSTART_SHIFT - RPB_ROWS
    g = jnp.pad(g, ((0, 0), (DSTART_SHIFT, hi_pad), (0, 0), (0, 0), (0, 0)))
    t = jnp.stack([g[:, j:j + N_DSTART] for j in range(ROWS_PER_LANE_TILE)], axis=4)
    t = jnp.transpose(t, (0, 2, 1, 3, 4, 5)).reshape(h, N_COL_BLOCKS, N_DSTART, Q_BLOCK_W, LANES)
    return t.reshape(h // HEADS_PER_TILE, HEADS_PER_TILE, N_COL_BLOCKS, N_DSTART, Q_BLOCK_W, LANES)


def _na_kernel(rows, blk_start, q_ref, k_ref, v_ref, kc_ref, vc_ref, t_ref, m_ref, o_ref):
    rb = pl.program_id(2)
    n_rb = pl.num_programs(2)
    kr0 = jnp.clip(rb * Q_ROWS - KH // 2, 0, rows - K_ROWS)
    cls = jnp.where(rb == 0, 0, jnp.where(rb == n_rb - 1, 2, 1))
    off = kr0 - rb * Q_ROWS + KH - 1 + DSTART_SHIFT
    lane = lax.broadcasted_iota(jnp.int32, (1, LANES), 1)
    kctx = kc_ref[0].astype(BF16)
    vctx = vc_ref[0].astype(BF16)
    n_groups = K_ROWS // ROWS_PER_LANE_TILE
    for n in range(N_COL_BLOCKS):
        c0 = int(blk_start[n])
        qn = jnp.concatenate(
            [q_ref[0, qr * GRID_W + n * Q_BLOCK_W: qr * GRID_W + (n + 1) * Q_BLOCK_W, :] for qr in range(Q_ROWS)],
            axis=0) * (HEAD_DIM ** -0.5)
        kn = jnp.concatenate(
            [k_ref[0, pl.ds(pl.multiple_of((kr0 + kr) * GRID_W + c0, SUBLANES), K_BLOCK_W), :] for kr in range(K_ROWS)],
            axis=0).astype(BF16)
        vn = jnp.concatenate(
            [v_ref[0, pl.ds(pl.multiple_of((kr0 + kr) * GRID_W + c0, SUBLANES), K_BLOCK_W), :] for kr in range(K_ROWS)],
            axis=0).astype(BF16)
        o_n = None
        for h in range(HEADS_PER_TILE):
            in_head = (lane >= h * HEAD_DIM) & (lane < (h + 1) * HEAD_DIM)
            qh = jnp.where(in_head, qn, 0.0).astype(BF16)
            bias = jnp.concatenate([
                jnp.concatenate([t_ref[0, h, n, off + ROWS_PER_LANE_TILE * g - qr] + m_ref[cls, n, qr, g]
                                 for g in range(n_groups)], axis=1)
                for qr in range(Q_ROWS)], axis=0)
            s_lat = _dot_nt(qh, kn) + bias
            s_ctx = _dot_nt(qh, kctx)
            m = jnp.maximum(jnp.max(s_lat, axis=-1, keepdims=True), jnp.max(s_ctx, axis=-1, keepdims=True))
            e_lat = jnp.exp(s_lat - m)
            e_ctx = jnp.exp(s_ctx - m)
            den = jnp.sum(e_lat, axis=-1, keepdims=True) + jnp.sum(e_ctx, axis=-1, keepdims=True)
            o_h = (_dot(e_lat.astype(BF16), vn) + _dot(e_ctx.astype(BF16), vctx)) / den
            o_n = o_h if o_n is None else jnp.where(in_head, o_h, o_n)
        for qr in range(Q_ROWS):
            o_ref[0, qr * GRID_W + n * Q_BLOCK_W: qr * GRID_W + (n + 1) * Q_BLOCK_W, :] = (
                o_n[qr * Q_BLOCK_W:(qr + 1) * Q_BLOCK_W])


def _neighbourhood_attention(proj, proj_c, bias_table, mask_table):
    b, s, d4 = proj.shape
    d = d4 // 4
    c = proj_c.shape[1]
    rows = s // GRID_W
    n_tiles = d // LANES
    blk_start, _, _ = _col_layout()
    tq = Q_ROWS * GRID_W
    return pl.pallas_call(
        functools.partial(_na_kernel, rows, blk_start),
        grid=(b, n_tiles, rows // Q_ROWS),
        in_specs=[
            pl.BlockSpec((1, tq, LANES), lambda bi, hp, rb: (bi, rb, hp)),
            pl.BlockSpec((1, s, LANES), lambda bi, hp, rb: (bi, 0, n_tiles + hp)),
            pl.BlockSpec((1, s, LANES), lambda bi, hp, rb: (bi, 0, 2 * n_tiles + hp)),
            pl.BlockSpec((1, c, LANES), lambda bi, hp, rb: (bi, 0, n_tiles + hp)),
            pl.BlockSpec((1, c, LANES), lambda bi, hp, rb: (bi, 0, 2 * n_tiles + hp)),
            pl.BlockSpec((1,) + bias_table.shape[1:], lambda bi, hp, rb: (hp, 0, 0, 0, 0, 0)),
            pl.BlockSpec(mask_table.shape, lambda bi, hp, rb: (0, 0, 0, 0, 0, 0)),
        ],
        out_specs=pl.BlockSpec((1, tq, LANES), lambda bi, hp, rb: (bi, rb, hp)),
        out_shape=jax.ShapeDtypeStruct((b, s, d), F32),
        compiler_params=_cparams(("arbitrary", "arbitrary", "arbitrary")),
        name="neighbourhood_attention",
    )(proj, proj, proj, proj_c, proj_c, bias_table, mask_table)


def _ctx_attn_kernel(q_ref, k_ref, v_ref, o_ref):
    lane = lax.broadcasted_iota(jnp.int32, (1, LANES), 1)
    q = q_ref[0] * (HEAD_DIM ** -0.5)
    k = k_ref[0].astype(BF16)
    v = v_ref[0].astype(BF16)
    out = None
    for h in range(HEADS_PER_TILE):
        in_head = (lane >= h * HEAD_DIM) & (lane < (h + 1) * HEAD_DIM)
        s = _dot_nt(jnp.where(in_head, q, 0.0).astype(BF16), k)
        e = jnp.exp(s - jnp.max(s, axis=-1, keepdims=True))
        o_h = _dot(e.astype(BF16), v) / jnp.sum(e, axis=-1, keepdims=True)
        out = o_h if out is None else jnp.where(in_head, o_h, out)
    o_ref[0] = out


def _context_attention(proj_c):
    b, c, d4 = proj_c.shape
    d = d4 // 4
    n_tiles = d // LANES
    return pl.pallas_call(
        _ctx_attn_kernel,
        grid=(b, n_tiles),
        in_specs=[
            pl.BlockSpec((1, c, LANES), lambda bi, hp: (bi, 0, hp)),
            pl.BlockSpec((1, c, LANES), lambda bi, hp: (bi, 0, n_tiles + hp)),
            pl.BlockSpec((1, c, LANES), lambda bi, hp: (bi, 0, 2 * n_tiles + hp)),
        ],
        out_specs=pl.BlockSpec((1, c, LANES), lambda bi, hp: (bi, 0, hp)),
        out_shape=jax.ShapeDtypeStruct((b, c, d), F32),
        compiler_params=_cparams(("arbitrary", "arbitrary")),
        name="context_attention",
    )(proj_c, proj_c, proj_c)


def _conv_block(length):
    return min(CONV_BLOCK, length)


def _acc_rows(nf):
    return max(r for r in range(SUBLANES, MAX_ACC_ROWS + 1, SUBLANES) if nf % r == 0)


def _dft_matrices(p):
    nf = p + FREQ_PAD
    k = np.arange(nf, dtype=np.float64)[:, None]
    keep = (k <= p).astype(np.float64)
    n = np.arange(p, dtype=np.float64)[None, :]
    ang = 2.0 * np.pi * k * n / (2 * p)
    fwd = np.concatenate([np.cos(ang) * keep, -np.sin(ang) * keep], axis=0)
    wk = np.where((k == 0) | (k == p), 1.0, 2.0) * keep / (2 * p)
    ang_i = 2.0 * np.pi * k * (n + p) / (2 * p)
    inv = np.concatenate([np.cos(ang_i) * wk, -np.sin(ang_i) * wk], axis=0).T
    return fwd.astype(np.float32), inv.astype(np.float32)


def _filter_kernel(length, p, z_ref, w1_ref, b1_ref, w2_ref, b2_ref, w3_ref, b3_ref, fr_ref, w4_ref,
                   dl_ref, fh_ref, fl_ref, o_ref, a_scr, t_scr, s_scr):
    nb = length // p
    nf = p + FREQ_PAD
    cb = o_ref.shape[-1]

    @pl.when((pl.program_id(0) == 0) & (pl.program_id(1) == 0))
    def _():
        a = jnp.sin(fr_ref[0:1, :] * (_dot3(z_ref[...], w1_ref[...]) + b1_ref[...]))
        a = jnp.sin(fr_ref[1:2, :] * (_dot3(a, w2_ref[...]) + b2_ref[...]))
        a_scr[...] = jnp.sin(fr_ref[2:3, :] * (_dot3(a, w3_ref[...]) + b3_ref[...]))

    lag = lax.broadcasted_iota(jnp.int32, (length, cb), 0)
    t_back = (length - lag).astype(F32) * (1.0 / (length - 1))
    t_fwd = lag.astype(F32) * (1.0 / (length - 1))
    back = _dot3(a_scr[0:length, :], w4_ref[0, 1]) * jnp.exp(-t_back * dl_ref[...])
    back = jnp.where(lag == 0, 0.0, back)
    fwd = _dot3(a_scr[length:2 * length, :], w4_ref[0, 0]) * jnp.exp(-t_fwd * dl_ref[...])
    inv_norm = 1.0 / (jnp.sum(jnp.abs(back), axis=0, keepdims=True) + jnp.sum(jnp.abs(fwd), axis=0, keepdims=True))
    t_scr[0:length, :] = back * inv_norm
    t_scr[length:2 * length, :] = fwd * inv_norm

    fh = fh_ref[...]
    fl = fl_ref[...]
    for blk in range(2 * nb):
        th, tl = _split_bf16(t_scr[blk * p:(blk + 1) * p, :])
        s_scr[blk] = _dot(fh, th) + _dot(fl, th) + _dot(fh, tl)
    krow = lax.broadcasted_iota(jnp.int32, (2 * nf, cb), 0)
    sign = jnp.where((krow & 1) == 0, 1.0, -1.0)
    for di in range(2 * nb - 1):
        o_ref[0, di] = s_scr[di] + sign * s_scr[di + 1]


def _hyena_filters(length, w1, b1, w2, b2, w3, b3, w4, freq):
    p = _conv_block(length)
    nb = length // p
    nf = p + FREQ_PAD
    width = w4.shape[-1] // 4
    cb = LANES
    t = jnp.linspace(0.0, 1.0, length, dtype=F32)[:, None]
    bands = (HY_EMB_DIM - 1) // 2
    wpos = 2.0 * math.pi * jnp.arange(length, dtype=F32)[:, None] / length
    f = jnp.linspace(1e-4, bands - 1, bands, dtype=F32)[None, :]
    z = jnp.concatenate([t, jnp.cos(f * wpos), -jnp.sin(f * wpos)], axis=-1)
    z_back = jnp.concatenate([jnp.zeros((1, HY_EMB_DIM), F32), z[1:][::-1]], axis=0)
    emb_pad = 64
    z_all = jnp.pad(jnp.concatenate([z_back, z], axis=0), ((0, 0), (0, emb_pad - HY_EMB_DIM)))
    w1p = jnp.pad(w1, ((0, emb_pad - HY_EMB_DIM), (0, 0)))
    order = w4.shape[-1] // (2 * width)
    w4r = jnp.transpose(w4.reshape(w4.shape[0], order, 2, width), (1, 2, 0, 3))
    max_decay = math.log(HY_TARGET) / HY_FAST_DECAY
    min_decay = math.log(HY_TARGET) / HY_SLOW_DECAY
    deltas = jnp.abs(jnp.linspace(min_decay, max_decay, width, dtype=F32)).reshape(1, width)
    fwd_np, _ = _dft_matrices(p)
    fwd = jnp.asarray(fwd_np)
    f_hi = fwd.astype(BF16)
    f_lo = (fwd - f_hi.astype(F32)).astype(BF16)
    hid = w2.shape[0]
    full = lambda shape: pl.BlockSpec(shape, lambda o, c: (0,) * len(shape))
    return pl.pallas_call(
        functools.partial(_filter_kernel, length, p),
        grid=(order, width // cb),
        in_specs=[
            full(z_all.shape), full(w1p.shape), full((1, hid)), full(w2.shape), full((1, hid)),
            full(w3.shape), full((1, hid)), full(freq.shape),
            pl.BlockSpec((1, 2, hid, cb), lambda o, c: (o, 0, 0, c)),
            pl.BlockSpec((1, cb), lambda o, c: (0, c)),
            full(f_hi.shape), full(f_lo.shape),
        ],
        out_specs=pl.BlockSpec((1, 2 * nb - 1, 2 * nf, cb), lambda o, c: (o, 0, 0, c)),
        out_shape=jax.ShapeDtypeStruct((order, 2 * nb - 1, 2 * nf, width), F32),
        scratch_shapes=[pltpu.VMEM((2 * length, hid), F32), pltpu.VMEM((2 * length, cb), F32),
                        pltpu.VMEM((2 * nb, 2 * nf, cb), F32)],
        compiler_params=_cparams(("arbitrary", "arbitrary")),
        name="hyena_filters",
    )(z_all, w1p, b1.reshape(1, hid), w2, b2.reshape(1, hid), w3, b3.reshape(1, hid), freq, w4r, deltas, f_hi, f_lo)


def _short_conv_block(pad_ref, j, p, w_ref, b_ref):
    base = SUBLANES + j * p
    return (pad_ref[base - 1: base - 1 + p, :] * w_ref[0:1, :] + pad_ref[base: base + p, :] * w_ref[1:2, :]
            + pad_ref[base + 1: base + 1 + p, :] * w_ref[2:3, :] + b_ref[...])


def _fill_padded(pad_ref, src_ref, length):
    zeros = jnp.zeros((SUBLANES, pad_ref.shape[1]), F32)
    pad_ref[0:SUBLANES, :] = zeros
    pad_ref[SUBLANES + length: 2 * SUBLANES + length, :] = zeros
    pad_ref[SUBLANES: SUBLANES + length, :] = src_ref[0]


def _conv_kernel(length, p, conv_u, u_ref, m_ref, h_ref, f_ref, g_ref, cwu_ref, cbu_ref, cwm_ref, cbm_ref,
                 skip_ref, o_ref, upad, mpad, u_scr, uhat, yhat):
    nb = length // p
    nf = p + FREQ_PAD
    ACC_ROWS = _acc_rows(nf)
    _fill_padded(mpad, m_ref, length)
    if conv_u:
        _fill_padded(upad, u_ref, length)
    fwd = f_ref[...]
    inv = g_ref[...]
    for j in range(nb):
        if conv_u:
            uj = _short_conv_block(upad, j, p, cwu_ref, cbu_ref)
        else:
            uj = u_ref[0, j * p:(j + 1) * p, :]
        u_scr[j * p:(j + 1) * p, :] = uj
        uhat[j] = _dot(fwd, uj.astype(BF16))
    for i in range(nb):
        def chunk(c, carry):
            r0 = pl.multiple_of(c * ACC_ROWS, SUBLANES)
            acc_r = jnp.zeros((ACC_ROWS, u_scr.shape[1]), F32)
            acc_i = jnp.zeros((ACC_ROWS, u_scr.shape[1]), F32)
            for j in range(nb):
                di = i - j + nb - 1
                hr = h_ref[0, di, pl.ds(r0, ACC_ROWS), :]
                hi = h_ref[0, di, pl.ds(nf + r0, ACC_ROWS), :]
                ur = uhat[j, pl.ds(r0, ACC_ROWS), :]
                ui = uhat[j, pl.ds(nf + r0, ACC_ROWS), :]
                acc_r = acc_r + (hr * ur - hi * ui)
                acc_i = acc_i + (hr * ui + hi * ur)
            yhat[pl.ds(r0, ACC_ROWS), :] = acc_r
            yhat[pl.ds(nf + r0, ACC_ROWS), :] = acc_i
            return carry

        lax.fori_loop(0, nf // ACC_ROWS, chunk, 0)
        y = _dot(inv, yhat[...].astype(BF16))
        mi = _short_conv_block(mpad, i, p, cwm_ref, cbm_ref)
        o_ref[0, i * p:(i + 1) * p, :] = mi * (y + u_scr[i * p:(i + 1) * p, :] * skip_ref[...])


def _long_conv_gate(u_src, u_col, conv_u, proj, m_col, spectra, order, conv_w, conv_b, skip):
    b, length, _ = proj.shape
    width = skip.shape[-1]
    p = _conv_block(length)
    nb = length // p
    nf = p + FREQ_PAD
    cb = LANES
    ncb = width // cb
    fwd_np, inv_np = _dft_matrices(p)
    fwd = jnp.asarray(fwd_np).astype(BF16)
    inv = jnp.asarray(inv_np).astype(BF16)
    cwu_blk = 0 if conv_u else 0
    col = lambda base: (lambda c, bi: (bi, 0, base * ncb + c))
    par = lambda base: (lambda c, bi: (0, base * ncb + c))
    return pl.pallas_call(
        functools.partial(_conv_kernel, length, p, conv_u),
        grid=(ncb, b),
        in_specs=[
            pl.BlockSpec((1, length, cb), col(u_col)),
            pl.BlockSpec((1, length, cb), col(m_col)),
            pl.BlockSpec((1, 2 * nb - 1, 2 * nf, cb), lambda c, bi: (order, 0, 0, c)),
            pl.BlockSpec(fwd.shape, lambda c, bi: (0, 0)),
            pl.BlockSpec(inv.shape, lambda c, bi: (0, 0)),
            pl.BlockSpec((3, cb), par(cwu_blk)),
            pl.BlockSpec((1, cb), par(cwu_blk)),
            pl.BlockSpec((3, cb), par(m_col)),
            pl.BlockSpec((1, cb), par(m_col)),
            pl.BlockSpec((1, cb), lambda c, bi: (0, c)),
        ],
        out_specs=pl.BlockSpec((1, length, cb), lambda c, bi: (bi, 0, c)),
        out_shape=jax.ShapeDtypeStruct((b, length, width), F32),
        scratch_shapes=[
            pltpu.VMEM((length + 2 * SUBLANES, cb), F32), pltpu.VMEM((length + 2 * SUBLANES, cb), F32),
            pltpu.VMEM((length, cb), F32), pltpu.VMEM((nb, 2 * nf, cb), F32), pltpu.VMEM((2 * nf, cb), F32),
        ],
        compiler_params=_cparams(("arbitrary", "arbitrary")),
        name="long_conv_gate",
    )(u_src, proj, spectra, fwd, inv, conv_w, conv_b.reshape(1, -1), conv_w, conv_b.reshape(1, -1),
      skip[order].reshape(1, width))


def _hyena_core(proj, spectra, conv_w, conv_b, skip):
    z = _long_conv_gate(proj, 0, True, proj, 1, spectra, 0, conv_w, conv_b, skip)
    return _long_conv_gate(z, 0, False, proj, 2, spectra, 1, conv_w, conv_b, skip)


def kernel(x, c, ctx, c_ctx, w_ada, b_ada, w_in, w_out, ln_g, ln_b, na_rpb, hy_conv_w, hy_conv_b, hy_f_w1, hy_f_b1,
           hy_f_w2, hy_f_b2, hy_f_w3, hy_f_b3, hy_f_w4, hy_f_freq, hy_skip):
    depth = w_in.shape[0]
    b, s, d = x.shape
    n_ctx = ctx.shape[1]
    n_mixers = 2
    alpha = (2 * depth) ** 0.25

    cc = jnp.concatenate([c, c_ctx[None, :], jnp.zeros((2 * SUBLANES - b - 1, d), F32)], axis=0)
    mod = _ada_vectors(cc, w_ada, b_ada)
    w_in_bf = w_in.astype(BF16)
    w_out_bf = w_out.astype(BF16)
    mask_table = jnp.asarray(_na_mask_table(s // GRID_W))

    for i in range(depth):
        last = i == depth - 1
        j = i // n_mixers
        shift, scale, gate = [mod[i, :b, None, k * d:(k + 1) * d] for k in range(3)]
        shift_c, scale_c, gate_c = [jnp.broadcast_to(mod[i, b, k * d:(k + 1) * d], (b, 1, d)) for k in range(3)]
        proj = _in_projection(x, scale, shift, w_in_bf[i])
        proj_c = None
        if i % n_mixers == 0 or not last:
            proj_c = _in_projection(ctx, scale_c, shift_c, w_in_bf[i])
        if i % n_mixers == 0:
            bias_table = _na_bias_table(na_rpb[j])
            a = _neighbourhood_attention(proj, proj_c, bias_table, mask_table)
            a_c = None if last else _context_attention(proj_c)
        else:
            filt = (hy_f_w1[j], hy_f_b1[j], hy_f_w2[j], hy_f_b2[j], hy_f_w3[j], hy_f_b3[j], hy_f_w4[j], hy_f_freq[j])
            a = _hyena_core(proj, _hyena_filters(s, *filt), hy_conv_w[j], hy_conv_b[j], hy_skip[j])
            a_c = None if last else _hyena_core(proj_c, _hyena_filters(n_ctx, *filt), hy_conv_w[j], hy_conv_b[j],
                                                hy_skip[j])
        x = _out_projection(a, proj, x, gate, w_out_bf[i], ln_g[i], ln_b[i], alpha)
        if not last:
            ctx = _out_projection(a_c, proj_c, ctx, gate_c, w_out_bf[i], ln_g[i], ln_b[i], alpha)
    return x
```

```python
import functools
import math

import numpy as np
import jax
import jax.numpy as jnp
from jax import lax
from jax.experimental import pallas as pl
from jax.experimental.pallas import tpu as pltpu

F32 = jnp.float32
BF16 = jnp.bfloat16

HEAD_DIM = 64
GRID_W = 64
KH = 8
KW = 16
Q_BLOCK_W = 16
K_BLOCK_W = Q_BLOCK_W + KW
N_COL_BLOCKS = GRID_W // Q_BLOCK_W
RPB_ROWS = 2 * KH - 1
RPB_COLS = 2 * KW - 1
HY_EMB_DIM = 33
HY_FAST_DECAY = 0.3
HY_SLOW_DECAY = 1.5
HY_TARGET = 1e-2
LN_EPS = 1e-5
NEG_INF = -1e30
LOG2E = 1.4426950408889634

LANES = 128
SUBLANES = 8
HEADS_PER_TILE = LANES // HEAD_DIM
VMEM_LIMIT = 56 * 1024 * 1024

Q_ROWS = 8
K_ROWS = 16
ROWS_PER_LANE_TILE = LANES // K_BLOCK_W
N_DSTART = 28
DSTART_SHIFT = 8
CONV_BLOCK = 512
CONV_BATCH = 2
ACC_ROWS = 32
FILTER_LANES = 256
IN_PROJ_ROWS = 512


def _cparams(sem):
    return pltpu.CompilerParams(dimension_semantics=sem, vmem_limit_bytes=VMEM_LIMIT)


def _split_bf16(a):
    hi = a.astype(BF16)
    lo = (a - hi.astype(F32)).astype(BF16)
    return hi, lo


def _dot(a, b):
    return jnp.dot(a, b, preferred_element_type=F32)


def _dot_nt(a, b):
    return lax.dot_general(a, b, (((1,), (1,)), ((), ())), preferred_element_type=F32)


def _dot3(a, b):
    ah, al = _split_bf16(a)
    bh, bl = _split_bf16(b)
    return _dot(ah, bh) + _dot(al, bh) + _dot(ah, bl)


def _silu(x):
    return x * jax.nn.sigmoid(x)


def _ada_kernel(cc_ref, w_ref, b_ref, o_ref):
    o_ref[0] = _dot3(_silu(cc_ref[...]), w_ref[0]) + b_ref[0]


def _ada_vectors(cc, w_ada, b_ada):
    depth, d, n = w_ada.shape
    tn = min(n, 1024)
    return pl.pallas_call(
        _ada_kernel,
        grid=(depth, n // tn),
        in_specs=[
            pl.BlockSpec(cc.shape, lambda i, j: (0, 0)),
            pl.BlockSpec((1, d, tn), lambda i, j: (i, 0, j)),
            pl.BlockSpec((1, 1, tn), lambda i, j: (i, 0, j)),
        ],
        out_specs=pl.BlockSpec((1, cc.shape[0], tn), lambda i, j: (i, 0, j)),
        out_shape=jax.ShapeDtypeStruct((depth, cc.shape[0], n), F32),
        compiler_params=_cparams(("arbitrary", "arbitrary")),
        name="ada_vectors",
    )(cc, w_ada, b_ada.reshape(depth, 1, n))


def _inproj_kernel(x_ref, sc_ref, sh_ref, w_ref, o_ref):
    h = (x_ref[0] * (1.0 + sc_ref[0]) + sh_ref[0]).astype(BF16)
    o_ref[0] = _dot(h, w_ref[...]).astype(o_ref.dtype)


def _in_projection(x, scale, shift, w_bf16, out_dtype):
    b, s, d = x.shape
    n = w_bf16.shape[1]
    tm = min(s, IN_PROJ_ROWS)
    return pl.pallas_call(
        _inproj_kernel,
        grid=(b, s // tm),
        in_specs=[
            pl.BlockSpec((1, tm, d), lambda bi, i: (bi, i, 0)),
            pl.BlockSpec((1, 1, d), lambda bi, i: (bi, 0, 0)),
            pl.BlockSpec((1, 1, d), lambda bi, i: (bi, 0, 0)),
            pl.BlockSpec((d, n), lambda bi, i: (0, 0), pipeline_mode=pl.Buffered(1)),
        ],
        out_specs=pl.BlockSpec((1, tm, n), lambda bi, i: (bi, i, 0)),
        out_shape=jax.ShapeDtypeStruct((b, s, n), out_dtype),
        compiler_params=_cparams(("arbitrary", "arbitrary")),
        name="in_projection",
    )(x, scale, shift, w_bf16)


def _outproj_kernel(alpha, a_ref, z_ref, x_ref, gate_ref, w_ref, g_ref, b_ref, o_ref):
    a = a_ref[0].astype(F32) * _silu(z_ref[0].astype(F32))
    y = _dot(a.astype(BF16), w_ref[...])
    r = alpha * x_ref[0] + gate_ref[0] * y
    mu = jnp.mean(r, axis=-1, keepdims=True)
    dlt = r - mu
    var = jnp.mean(dlt * dlt, axis=-1, keepdims=True)
    o_ref[0] = dlt * lax.rsqrt(var + LN_EPS) * g_ref[...] + b_ref[...]


def _out_projection(a, proj, x, gate, w_bf16, ln_g, ln_b, alpha):
    b, s, d = x.shape
    tm = min(s, 512)
    row = lambda bi, i: (bi, i, 0)
    return pl.pallas_call(
        functools.partial(_outproj_kernel, alpha),
        grid=(b, s // tm),
        in_specs=[
            pl.BlockSpec((1, tm, d), row),
            pl.BlockSpec((1, tm, d), lambda bi, i: (bi, i, 3)),
            pl.BlockSpec((1, tm, d), row),
            pl.BlockSpec((1, 1, d), lambda bi, i: (bi, 0, 0)),
            pl.BlockSpec((d, d), lambda bi, i: (0, 0)),
            pl.BlockSpec((1, d), lambda bi, i: (0, 0)),
            pl.BlockSpec((1, d), lambda bi, i: (0, 0)),
        ],
        out_specs=pl.BlockSpec((1, tm, d), row),
        out_shape=jax.ShapeDtypeStruct((b, s, d), F32),
        compiler_params=_cparams(("arbitrary", "arbitrary")),
        name="out_projection",
    )(a, proj, x, gate, w_bf16, ln_g.reshape(1, d), ln_b.reshape(1, d))


def _col_layout():
    q_cols = np.arange(GRID_W).reshape(N_COL_BLOCKS, Q_BLOCK_W)
    q_start = np.clip(q_cols - KW // 2, 0, GRID_W - KW)
    blk_start = np.clip(np.arange(N_COL_BLOCKS) * Q_BLOCK_W - KW // 2, 0, GRID_W - K_BLOCK_W)
    k_cols = blk_start[:, None] + np.arange(K_BLOCK_W)
    kc = k_cols[:, None, :]
    in_win = (kc >= q_start[:, :, None]) & (kc < q_start[:, :, None] + KW)
    dcol = np.clip(kc - q_cols[:, :, None] + KW - 1, 0, RPB_COLS - 1)
    return blk_start, in_win, dcol


def _window_base(rb, rows):
    return np.clip(rb * Q_ROWS - KH // 2, 0, rows - K_ROWS)


def _na_mask_table(rows):
    _, in_win, _ = _col_layout()
    n_rb = rows // Q_ROWS
    out = np.zeros((3, N_COL_BLOCKS, Q_ROWS, K_ROWS // ROWS_PER_LANE_TILE, Q_BLOCK_W, LANES), np.float32)
    for cls, rb in enumerate((0, 1, n_rb - 1)):
        kr0 = _window_base(rb, rows)
        for qr in range(Q_ROWS):
            r = rb * Q_ROWS + qr
            r0 = np.clip(r - KH // 2, 0, rows - KH)
            for kr in range(K_ROWS):
                row_ok = r0 <= kr0 + kr < r0 + KH
                g, j = divmod(kr, ROWS_PER_LANE_TILE)
                ok = in_win & row_ok
                out[cls, :, qr, g, :, j * K_BLOCK_W:(j + 1) * K_BLOCK_W] = np.where(ok, 0.0, NEG_INF)
    return out


def _rpb_expand_kernel(r_ref, e_ref, o_ref):
    r = r_ref[...]
    hi = r.astype(BF16)
    r1 = r - hi.astype(F32)
    mid = r1.astype(BF16)
    lo = (r1 - mid.astype(F32)).astype(BF16)
    e = e_ref[...]
    o_ref[...] = (_dot(hi, e) + _dot(mid, e) + _dot(lo, e)) * LOG2E


def _na_bias_table(rpb):
    h = rpb.shape[0]
    _, _, dcol = _col_layout()
    kpad = 32
    onehot = (dcol.reshape(-1)[None, :] == np.arange(kpad)[:, None]).astype(np.float32)
    r2 = jnp.pad(rpb.reshape(h * RPB_ROWS, RPB_COLS), ((0, 0), (0, kpad - RPB_COLS)))
    mrows = -(-r2.shape[0] // 16) * 16
    r2 = jnp.pad(r2, ((0, mrows - r2.shape[0]), (0, 0)))
    g = pl.pallas_call(
        _rpb_expand_kernel,
        out_shape=jax.ShapeDtypeStruct((mrows, onehot.shape[1]), F32),
        name="rpb_expand",
    )(r2, jnp.asarray(onehot, BF16))
    g = g[:h * RPB_ROWS].reshape(h, RPB_ROWS, N_COL_BLOCKS, Q_BLOCK_W, K_BLOCK_W)
    hi_pad = N_DSTART + ROWS_PER_LANE_TILE - 1 - DSTART_SHIFT - RPB_ROWS
    g = jnp.pad(g, ((0, 0), (DSTART_SHIFT, hi_pad), (0, 0), (0, 0), (0, 0)))
    t = jnp.stack([g[:, j:j + N_DSTART] for j in range(ROWS_PER_LANE_TILE)], axis=4)
    t = jnp.transpose(t, (0, 2, 1, 3, 4, 5)).reshape(h, N_COL_BLOCKS, N_DSTART, Q_BLOCK_W, LANES)
    return t.reshape(h // HEADS_PER_TILE, HEADS_PER_TILE, N_COL_BLOCKS, N_DSTART, Q_BLOCK_W, LANES)


def _na_kernel(rows, blk_start, q_ref, k_ref, v_ref, kc_ref, vc_ref, t_ref, m_ref, o_ref):
    rb = pl.program_id(2)
    n_rb = pl.num_programs(2)
    kr0 = jnp.clip(rb * Q_ROWS - KH // 2, 0, rows - K_ROWS)
    cls = jnp.where(rb == 0, 0, jnp.where(rb == n_rb - 1, 2, 1))
    off = kr0 - rb * Q_ROWS + KH - 1 + DSTART_SHIFT
    lane = lax.broadcasted_iota(jnp.int32, (1, LANES), 1)
    kctx = kc_ref[0].astype(BF16)
    vctx = vc_ref[0].astype(BF16)
    n_groups = K_ROWS // ROWS_PER_LANE_TILE
    nq = Q_ROWS * Q_BLOCK_W

    def scores(n):
        c0 = int(blk_start[n])
        qn = jnp.concatenate(
            [q_ref[0, qr * GRID_W + n * Q_BLOCK_W: qr * GRID_W + (n + 1) * Q_BLOCK_W, :] for qr in range(Q_ROWS)],
            axis=0) * (HEAD_DIM ** -0.5 * LOG2E)
        q2 = jnp.concatenate(
            [jnp.where((lane >= h * HEAD_DIM) & (lane < (h + 1) * HEAD_DIM), qn, 0.0) for h in range(HEADS_PER_TILE)],
            axis=0).astype(BF16)
        window = lambda ref: jnp.concatenate(
            [ref[0, pl.ds(pl.multiple_of((kr0 + kr) * GRID_W + c0, SUBLANES), K_BLOCK_W), :] for kr in range(K_ROWS)],
            axis=0).astype(BF16)
        bias = jnp.concatenate([
            jnp.concatenate([t_ref[0, h, n, off + ROWS_PER_LANE_TILE * g - qr] + m_ref[cls, n, qr, g]
                             for g in range(n_groups)], axis=1)
            for h in range(HEADS_PER_TILE) for qr in range(Q_ROWS)], axis=0)
        return _dot_nt(q2, window(k_ref)) + bias, _dot_nt(q2, kctx), window(v_ref)

    def finish(n, s_lat, s_ctx, vn):
        m = jnp.maximum(jnp.max(s_lat, axis=-1, keepdims=True), jnp.max(s_ctx, axis=-1, keepdims=True))
        e_lat = jnp.exp2(s_lat - m)
        e_ctx = jnp.exp2(s_ctx - m)
        den = jnp.sum(e_lat, axis=-1, keepdims=True) + jnp.sum(e_ctx, axis=-1, keepdims=True)
        o2 = (_dot(e_lat.astype(BF16), vn) + _dot(e_ctx.astype(BF16), vctx)) / den
        o_n = o2[0:nq]
        for h in range(1, HEADS_PER_TILE):
            o_n = jnp.where((lane >= h * HEAD_DIM) & (lane < (h + 1) * HEAD_DIM), o2[h * nq:(h + 1) * nq], o_n)
        for qr in range(Q_ROWS):
            o_ref[0, qr * GRID_W + n * Q_BLOCK_W: qr * GRID_W + (n + 1) * Q_BLOCK_W, :] = (
                o_n[qr * Q_BLOCK_W:(qr + 1) * Q_BLOCK_W].astype(o_ref.dtype))

    cur = scores(0)
    for n in range(N_COL_BLOCKS):
        nxt = scores(n + 1) if n + 1 < N_COL_BLOCKS else None
        finish(n, *cur)
        cur = nxt


def _neighbourhood_attention(proj, proj_c, bias_table, mask_table):
    b, s, d4 = proj.shape
    d = d4 // 4
    c = proj_c.shape[1]
    rows = s // GRID_W
    n_tiles = d // LANES
    blk_start, _, _ = _col_layout()
    tq = Q_ROWS * GRID_W
    return pl.pallas_call(
        functools.partial(_na_kernel, rows, blk_start),
        grid=(b, n_tiles, rows // Q_ROWS),
        in_specs=[
            pl.BlockSpec((1, tq, LANES), lambda bi, hp, rb: (bi, rb, hp)),
            pl.BlockSpec((1, s, LANES), lambda bi, hp, rb: (bi, 0, n_tiles + hp)),
            pl.BlockSpec((1, s, LANES), lambda bi, hp, rb: (bi, 0, 2 * n_tiles + hp)),
            pl.BlockSpec((1, c, LANES), lambda bi, hp, rb: (bi, 0, n_tiles + hp)),
            pl.BlockSpec((1, c, LANES), lambda bi, hp, rb: (bi, 0, 2 * n_tiles + hp)),
            pl.BlockSpec((1,) + bias_table.shape[1:], lambda bi, hp, rb: (hp, 0, 0, 0, 0, 0)),
            pl.BlockSpec(mask_table.shape, lambda bi, hp, rb: (0, 0, 0, 0, 0, 0)),
        ],
        out_specs=pl.BlockSpec((1, tq, LANES), lambda bi, hp, rb: (bi, rb, hp)),
        out_shape=jax.ShapeDtypeStruct((b, s, d), BF16),
        compiler_params=_cparams(("arbitrary", "arbitrary", "arbitrary")),
        name="neighbourhood_attention",
    )(proj, proj, proj, proj_c, proj_c, bias_table, mask_table)


def _ctx_attn_kernel(q_ref, k_ref, v_ref, o_ref):
    lane = lax.broadcasted_iota(jnp.int32, (1, LANES), 1)
    q = q_ref[0] * (HEAD_DIM ** -0.5 * LOG2E)
    k = k_ref[0].astype(BF16)
    v = v_ref[0].astype(BF16)
    out = None
    for h in range(HEADS_PER_TILE):
        in_head = (lane >= h * HEAD_DIM) & (lane < (h + 1) * HEAD_DIM)
        s = _dot_nt(jnp.where(in_head, q, 0.0).astype(BF16), k)
        e = jnp.exp2(s - jnp.max(s, axis=-1, keepdims=True))
        o_h = _dot(e.astype(BF16), v) / jnp.sum(e, axis=-1, keepdims=True)
        out = o_h if out is None else jnp.where(in_head, o_h, out)
    o_ref[0] = out.astype(o_ref.dtype)


def _context_attention(proj_c):
    b, c, d4 = proj_c.shape
    d = d4 // 4
    n_tiles = d // LANES
    return pl.pallas_call(
        _ctx_attn_kernel,
        grid=(b, n_tiles),
        in_specs=[
            pl.BlockSpec((1, c, LANES), lambda bi, hp: (bi, 0, hp)),
            pl.BlockSpec((1, c, LANES), lambda bi, hp: (bi, 0, n_tiles + hp)),
            pl.BlockSpec((1, c, LANES), lambda bi, hp: (bi, 0, 2 * n_tiles + hp)),
        ],
        out_specs=pl.BlockSpec((1, c, LANES), lambda bi, hp: (bi, 0, hp)),
        out_shape=jax.ShapeDtypeStruct((b, c, d), BF16),
        compiler_params=_cparams(("arbitrary", "arbitrary")),
        name="context_attention",
    )(proj_c, proj_c, proj_c)


def _conv_block(length):
    return min(CONV_BLOCK, length)


def _dft_matrices(p):
    k = np.arange(p, dtype=np.float64)[:, None]
    n = np.arange(p, dtype=np.float64)[None, :]
    ang = 2.0 * np.pi * k * n / (2 * p)
    f_im = -np.sin(ang)
    f_im[0] = np.cos(np.pi * n[0])
    fwd = np.concatenate([np.cos(ang), f_im], axis=0)
    wk = np.where(k == 0, 1.0, 2.0) / (2 * p)
    ang_i = 2.0 * np.pi * k * (n + p) / (2 * p)
    g_im = -np.sin(ang_i) * wk
    g_im[0] = np.cos(np.pi * (n[0] + p)) / (2 * p)
    inv = np.concatenate([np.cos(ang_i) * wk, g_im], axis=0).T
    return fwd.astype(np.float32), inv.astype(np.float32)


def _filter_kernel(length, p, z_ref, w1_ref, b1_ref, w2_ref, b2_ref, w3_ref, b3_ref, fr_ref, w4_ref,
                   dl_ref, f_ref, o_ref, a_scr, t_scr, prev_scr, norm_scr):
    nb = length // p
    di = pl.program_id(2)
    cb = o_ref.shape[-1]

    @pl.when((pl.program_id(0) == 0) & (pl.program_id(1) == 0) & (di == 0))
    def _():
        a = jnp.sin(fr_ref[0:1, :] * (_dot3(z_ref[...], w1_ref[...]) + b1_ref[...]))
        a = jnp.sin(fr_ref[1:2, :] * (_dot3(a, w2_ref[...]) + b2_ref[...]))
        a_scr[...] = jnp.sin(fr_ref[2:3, :] * (_dot3(a, w3_ref[...]) + b3_ref[...]))

    def block_dft(blk):
        start = pl.multiple_of(blk * p, p)
        return _dot(f_ref[...], t_scr[pl.ds(start, p), :].astype(BF16))

    @pl.when(di == 0)
    def _():
        total = jnp.zeros((1, cb), F32)
        for blk in range(2 * nb):
            back = blk < nb
            idx = lax.broadcasted_iota(jnp.int32, (p, cb), 0) + (blk % nb) * p
            t = ((length - idx) if back else idx).astype(F32) * (1.0 / (length - 1))
            taps = _dot3(a_scr[blk * p:(blk + 1) * p, :], w4_ref[0, 1 if back else 0]) * jnp.exp(-t * dl_ref[...])
            if blk == 0:
                taps = jnp.where(idx == 0, 0.0, taps)
            t_scr[blk * p:(blk + 1) * p, :] = taps
            total = total + jnp.sum(jnp.abs(taps), axis=0, keepdims=True)
        norm_scr[...] = jnp.broadcast_to(1.0 / total, norm_scr.shape)
        prev_scr[...] = block_dft(0)

    nxt = block_dft(di + 1)
    krow = lax.broadcasted_iota(jnp.int32, (2 * p, cb), 0)
    o_ref[0, 0] = (prev_scr[...] + jnp.where((krow & 1) == 0, nxt, -nxt)) * norm_scr[0:1, :]
    prev_scr[...] = nxt


def _hyena_filters(length, w1, b1, w2, b2, w3, b3, w4, freq):
    p = _conv_block(length)
    nb = length // p
    width = w4.shape[-1] // 4
    cb = min(FILTER_LANES, width)
    t = jnp.linspace(0.0, 1.0, length, dtype=F32)[:, None]
    bands = (HY_EMB_DIM - 1) // 2
    wpos = 2.0 * math.pi * jnp.arange(length, dtype=F32)[:, None] / length
    f = jnp.linspace(1e-4, bands - 1, bands, dtype=F32)[None, :]
    z = jnp.concatenate([t, jnp.cos(f * wpos), -jnp.sin(f * wpos)], axis=-1)
    z_back = jnp.concatenate([jnp.zeros((1, HY_EMB_DIM), F32), z[1:][::-1]], axis=0)
    emb_pad = 64
    z_all = jnp.pad(jnp.concatenate([z_back, z], axis=0), ((0, 0), (0, emb_pad - HY_EMB_DIM)))
    w1p = jnp.pad(w1, ((0, emb_pad - HY_EMB_DIM), (0, 0)))
    order = w4.shape[-1] // (2 * width)
    w4r = jnp.transpose(w4.reshape(w4.shape[0], order, 2, width), (1, 2, 0, 3))
    max_decay = math.log(HY_TARGET) / HY_FAST_DECAY
    min_decay = math.log(HY_TARGET) / HY_SLOW_DECAY
    deltas = jnp.abs(jnp.linspace(min_decay, max_decay, width, dtype=F32)).reshape(1, width)
    fwd = jnp.asarray(_dft_matrices(p)[0]).astype(BF16)
    hid = w2.shape[0]
    full = lambda shape: pl.BlockSpec(shape, lambda o, c, di: (0,) * len(shape))
    return pl.pallas_call(
        functools.partial(_filter_kernel, length, p),
        grid=(order, width // cb, 2 * nb - 1),
        in_specs=[
            full(z_all.shape), full(w1p.shape), full((1, hid)), full(w2.shape), full((1, hid)),
            full(w3.shape), full((1, hid)), full(freq.shape),
            pl.BlockSpec((1, 2, hid, cb), lambda o, c, di: (o, 0, 0, c)),
            pl.BlockSpec((1, cb), lambda o, c, di: (0, c)),
            full(fwd.shape),
        ],
        out_specs=pl.BlockSpec((1, 1, 2 * p, cb), lambda o, c, di: (o, di, 0, c)),
        out_shape=jax.ShapeDtypeStruct((order, 2 * nb - 1, 2 * p, width), F32),
        scratch_shapes=[pltpu.VMEM((2 * length, hid), F32), pltpu.VMEM((2 * length, cb), F32),
                        pltpu.VMEM((2 * p, cb), F32), pltpu.VMEM((SUBLANES, cb), F32)],
        compiler_params=_cparams(("arbitrary", "arbitrary", "arbitrary")),
        name="hyena_filters",
    )(z_all, w1p, b1.reshape(1, hid), w2, b2.reshape(1, hid), w3, b3.reshape(1, hid), freq, w4r, deltas, fwd)


def _short_conv_block(pad_ref, j, p, w_ref, b_ref):
    base = SUBLANES + j * p
    return (pad_ref[base - 1: base - 1 + p, :] * w_ref[0:1, :] + pad_ref[base: base + p, :] * w_ref[1:2, :]
            + pad_ref[base + 1: base + 1 + p, :] * w_ref[2:3, :] + b_ref[...])


def _fill_padded(pad_ref, src_ref, length):
    zeros = jnp.zeros((SUBLANES, pad_ref.shape[1]), F32)
    pad_ref[0:SUBLANES, :] = zeros
    pad_ref[SUBLANES + length: 2 * SUBLANES + length, :] = zeros
    pad_ref[SUBLANES: SUBLANES + length, :] = src_ref[...].astype(F32)


def _conv_kernel(length, p, conv_u, u_ref, m_ref, h_ref, f_ref, g_ref, cwu_ref, cbu_ref, cwm_ref, cbm_ref,
                 skip_ref, o_ref, upad, mpad, uhat, yhat):
    nb = length // p
    nbat = u_ref.shape[0]
    for bi in range(nbat):
        _fill_padded(mpad.at[bi], m_ref.at[bi], length)
        if conv_u:
            _fill_padded(upad.at[bi], u_ref.at[bi], length)

    def u_block(bi, j):
        if conv_u:
            return _short_conv_block(upad.at[bi], j, p, cwu_ref, cbu_ref)
        return u_ref[bi, j * p:(j + 1) * p, :]

    fwd = f_ref[...]
    for j in range(nb):
        uhat[j] = _dot(fwd, jnp.concatenate([u_block(bi, j).astype(BF16) for bi in range(nbat)], axis=1))

    inv = g_ref[...]
    for i in range(nb):
        def chunk(c, carry):
            r0 = pl.multiple_of(c * ACC_ROWS, ACC_ROWS)
            acc = [[jnp.zeros((ACC_ROWS, LANES), F32) for _ in range(2)] for _ in range(nbat)]
            for j in range(nb):
                di = i - j + nb - 1
                ha = h_ref[0, di, pl.ds(r0, ACC_ROWS), :]
                hb = h_ref[0, di, pl.ds(p + r0, ACC_ROWS), :]
                for bi in range(nbat):
                    ua = uhat[j, pl.ds(r0, ACC_ROWS), bi * LANES:(bi + 1) * LANES]
                    ub = uhat[j, pl.ds(p + r0, ACC_ROWS), bi * LANES:(bi + 1) * LANES]
                    acc[bi][0] = acc[bi][0] + (ha * ua - hb * ub)
                    acc[bi][1] = acc[bi][1] + (ha * ub + hb * ua)
            for bi in range(nbat):
                yhat[pl.ds(r0, ACC_ROWS), bi * LANES:(bi + 1) * LANES] = acc[bi][0]
                yhat[pl.ds(p + r0, ACC_ROWS), bi * LANES:(bi + 1) * LANES] = acc[bi][1]
            return carry

        lax.fori_loop(0, p // ACC_ROWS, chunk, 0)
        dc = jnp.zeros((SUBLANES, nbat * LANES), F32)
        ny = jnp.zeros((SUBLANES, nbat * LANES), F32)
        for j in range(nb):
            di = i - j + nb - 1
            dc = dc + jnp.concatenate([h_ref[0, di, 0:SUBLANES, :]] * nbat, axis=1) * uhat[j, 0:SUBLANES, :]
            ny = ny + jnp.concatenate([h_ref[0, di, p:p + SUBLANES, :]] * nbat, axis=1) * uhat[j, p:p + SUBLANES, :]
        first = lax.broadcasted_iota(jnp.int32, (SUBLANES, nbat * LANES), 0) == 0
        yhat[0:SUBLANES, :] = jnp.where(first, dc, yhat[0:SUBLANES, :])
        yhat[p:p + SUBLANES, :] = jnp.where(first, ny, yhat[p:p + SUBLANES, :])

        y = _dot(inv, yhat[...].astype(BF16))
        for bi in range(nbat):
            mi = _short_conv_block(mpad.at[bi], i, p, cwm_ref, cbm_ref)
            yi = y[:, bi * LANES:(bi + 1) * LANES] + u_block(bi, i).astype(F32) * skip_ref[...]
            o_ref[bi, i * p:(i + 1) * p, :] = (mi * yi).astype(o_ref.dtype)


def _long_conv_gate(u_src, u_col, conv_u, proj, m_col, spectra, order, conv_w, conv_b, skip, out_dtype):
    b, length, _ = proj.shape
    width = skip.shape[-1]
    p = _conv_block(length)
    nb = length // p
    cb = LANES
    ncb = width // cb
    nbat = CONV_BATCH
    assert b % nbat == 0 and p % ACC_ROWS == 0
    fwd_np, inv_np = _dft_matrices(p)
    fwd = jnp.asarray(fwd_np).astype(BF16)
    inv = jnp.asarray(inv_np).astype(BF16)
    col = lambda base: (lambda c, bi: (bi, 0, base * ncb + c))
    par = lambda base: (lambda c, bi: (0, base * ncb + c))
    once = pl.Buffered(1)
    return pl.pallas_call(
        functools.partial(_conv_kernel, length, p, conv_u),
        grid=(ncb, b // nbat),
        in_specs=[
            pl.BlockSpec((nbat, length, cb), col(u_col)),
            pl.BlockSpec((nbat, length, cb), col(m_col)),
            pl.BlockSpec((1, 2 * nb - 1, 2 * p, cb), lambda c, bi: (order, 0, 0, c), pipeline_mode=once),
            pl.BlockSpec(fwd.shape, lambda c, bi: (0, 0), pipeline_mode=once),
            pl.BlockSpec(inv.shape, lambda c, bi: (0, 0), pipeline_mode=once),
            pl.BlockSpec((3, cb), par(0)),
            pl.BlockSpec((1, cb), par(0)),
            pl.BlockSpec((3, cb), par(m_col)),
            pl.BlockSpec((1, cb), par(m_col)),
            pl.BlockSpec((1, cb), lambda c, bi: (0, c)),
        ],
        out_specs=pl.BlockSpec((nbat, length, cb), lambda c, bi: (bi, 0, c)),
        out_shape=jax.ShapeDtypeStruct((b, length, width), out_dtype),
        scratch_shapes=[
            pltpu.VMEM((nbat, length + 2 * SUBLANES, cb), F32), pltpu.VMEM((nbat, length + 2 * SUBLANES, cb), F32),
            pltpu.VMEM((nb, 2 * p, nbat * cb), F32), pltpu.VMEM((2 * p, nbat * cb), F32),
        ],
        compiler_params=_cparams(("arbitrary", "arbitrary")),
        name="long_conv_gate",
    )(u_src, proj, spectra, fwd, inv, conv_w, conv_b.reshape(1, -1), conv_w, conv_b.reshape(1, -1),
      skip[order].reshape(1, width))


def _hyena_core(proj, spectra, conv_w, conv_b, skip):
    z = _long_conv_gate(proj, 0, True, proj, 1, spectra, 0, conv_w, conv_b, skip, F32)
    return _long_conv_gate(z, 0, False, proj, 2, spectra, 1, conv_w, conv_b, skip, BF16)


def kernel(x, c, ctx, c_ctx, w_ada, b_ada, w_in, w_out, ln_g, ln_b, na_rpb, hy_conv_w, hy_conv_b, hy_f_w1, hy_f_b1,
           hy_f_w2, hy_f_b2, hy_f_w3, hy_f_b3, hy_f_w4, hy_f_freq, hy_skip):
    depth = w_in.shape[0]
    b, s, d = x.shape
    n_ctx = ctx.shape[1]
    n_mixers = 2
    alpha = (2 * depth) ** 0.25

    cc = jnp.concatenate([c, c_ctx[None, :], jnp.zeros((2 * SUBLANES - b - 1, d), F32)], axis=0)
    mod = _ada_vectors(cc, w_ada, b_ada)
    w_in_bf = w_in.astype(BF16)
    w_out_bf = w_out.astype(BF16)
    mask_table = jnp.asarray(_na_mask_table(s // GRID_W))

    for i in range(depth):
        last = i == depth - 1
        j = i // n_mixers
        shift, scale, gate = [mod[i, :b, None, k * d:(k + 1) * d] for k in range(3)]
        shift_c, scale_c, gate_c = [jnp.broadcast_to(mod[i, b, k * d:(k + 1) * d], (b, 1, d)) for k in range(3)]
        proj_dtype = F32 if i % n_mixers == 0 else BF16
        proj = _in_projection(x, scale, shift, w_in_bf[i], proj_dtype)
        proj_c = None
        if i % n_mixers == 0 or not last:
            proj_c = _in_projection(ctx, scale_c, shift_c, w_in_bf[i], proj_dtype)
        if i % n_mixers == 0:
            bias_table = _na_bias_table(na_rpb[j])
            a = _neighbourhood_attention(proj, proj_c, bias_table, mask_table)
            a_c = None if last else _context_attention(proj_c)
        else:
            filt = (hy_f_w1[j], hy_f_b1[j], hy_f_w2[j], hy_f_b2[j], hy_f_w3[j], hy_f_b3[j], hy_f_w4[j], hy_f_freq[j])
            a = _hyena_core(proj, _hyena_filters(s, *filt), hy_conv_w[j], hy_conv_b[j], hy_skip[j])
            a_c = None if last else _hyena_core(proj_c, _hyena_filters(n_ctx, *filt), hy_conv_w[j], hy_conv_b[j],
                                                hy_skip[j])
        x = _out_projection(a, proj, x, gate, w_out_bf[i], ln_g[i], ln_b[i], alpha)
        if not last:
            ctx = _out_projection(a_c, proj_c, ctx, gate_c, w_out_bf[i], ln_g[i], ln_b[i], alpha)
    return x
```

```python
import functools
import math

import numpy as np
import jax
import jax.numpy as jnp
from jax import lax
from jax.experimental import pallas as pl
from jax.experimental.pallas import tpu as pltpu

F32 = jnp.float32
BF16 = jnp.bfloat16

HEAD_DIM = 64
GRID_W = 64
KH = 8
KW = 16
Q_BLOCK_W = 16
K_BLOCK_W = Q_BLOCK_W + KW
N_COL_BLOCKS = GRID_W // Q_BLOCK_W
RPB_ROWS = 2 * KH - 1
RPB_COLS = 2 * KW - 1
HY_EMB_DIM = 33
HY_FAST_DECAY = 0.3
HY_SLOW_DECAY = 1.5
HY_TARGET = 1e-2
LN_EPS = 1e-5
NEG_INF = -1e30
LOG2E = 1.4426950408889634

LANES = 128
SUBLANES = 8
HEADS_PER_TILE = LANES // HEAD_DIM
VMEM_LIMIT = 56 * 1024 * 1024

Q_ROWS = 8
K_ROWS = 16
ROWS_PER_LANE_TILE = LANES // K_BLOCK_W
N_DSTART = 28
DSTART_SHIFT = 8
CONV_BLOCK = 1024
CONV_BATCH = 2
ACC_ROWS = 32
FILTER_LANES = 256
IN_PROJ_ROWS = 512


def _cparams(sem):
    return pltpu.CompilerParams(dimension_semantics=sem, vmem_limit_bytes=VMEM_LIMIT)


def _split_bf16(a):
    hi = a.astype(BF16)
    lo = (a - hi.astype(F32)).astype(BF16)
    return hi, lo


def _dot(a, b):
    return jnp.dot(a, b, preferred_element_type=F32)


def _dot_nt(a, b):
    return lax.dot_general(a, b, (((1,), (1,)), ((), ())), preferred_element_type=F32)


def _dot3(a, b):
    ah, al = _split_bf16(a)
    bh, bl = _split_bf16(b)
    return _dot(ah, bh) + _dot(al, bh) + _dot(ah, bl)


def _silu(x):
    return x * jax.nn.sigmoid(x)


def _ada_kernel(cc_ref, w_ref, b_ref, o_ref):
    o_ref[0] = _dot3(_silu(cc_ref[...]), w_ref[0]) + b_ref[0]


def _ada_vectors(cc, w_ada, b_ada):
    depth, d, n = w_ada.shape
    tn = min(n, 1024)
    return pl.pallas_call(
        _ada_kernel,
        grid=(depth, n // tn),
        in_specs=[
            pl.BlockSpec(cc.shape, lambda i, j: (0, 0)),
            pl.BlockSpec((1, d, tn), lambda i, j: (i, 0, j)),
            pl.BlockSpec((1, 1, tn), lambda i, j: (i, 0, j)),
        ],
        out_specs=pl.BlockSpec((1, cc.shape[0], tn), lambda i, j: (i, 0, j)),
        out_shape=jax.ShapeDtypeStruct((depth, cc.shape[0], n), F32),
        compiler_params=_cparams(("arbitrary", "arbitrary")),
        name="ada_vectors",
    )(cc, w_ada, b_ada.reshape(depth, 1, n))


def _inproj_kernel(x_ref, sc_ref, sh_ref, w_ref, o_ref):
    h = (x_ref[0] * (1.0 + sc_ref[0]) + sh_ref[0]).astype(BF16)
    o_ref[0] = _dot(h, w_ref[...]).astype(o_ref.dtype)


def _in_projection(x, scale, shift, w_bf16, out_dtype):
    b, s, d = x.shape
    n = w_bf16.shape[1]
    tm = min(s, IN_PROJ_ROWS)
    return pl.pallas_call(
        _inproj_kernel,
        grid=(b, s // tm),
        in_specs=[
            pl.BlockSpec((1, tm, d), lambda bi, i: (bi, i, 0)),
            pl.BlockSpec((1, 1, d), lambda bi, i: (bi, 0, 0)),
            pl.BlockSpec((1, 1, d), lambda bi, i: (bi, 0, 0)),
            pl.BlockSpec((d, n), lambda bi, i: (0, 0), pipeline_mode=pl.Buffered(1)),
        ],
        out_specs=pl.BlockSpec((1, tm, n), lambda bi, i: (bi, i, 0)),
        out_shape=jax.ShapeDtypeStruct((b, s, n), out_dtype),
        compiler_params=_cparams(("arbitrary", "arbitrary")),
        name="in_projection",
    )(x, scale, shift, w_bf16)


def _outproj_kernel(alpha, a_ref, z_ref, x_ref, gate_ref, w_ref, g_ref, b_ref, o_ref):
    a = a_ref[0].astype(F32) * _silu(z_ref[0].astype(F32))
    y = _dot(a.astype(BF16), w_ref[...])
    r = alpha * x_ref[0] + gate_ref[0] * y
    mu = jnp.mean(r, axis=-1, keepdims=True)
    dlt = r - mu
    var = jnp.mean(dlt * dlt, axis=-1, keepdims=True)
    o_ref[0] = dlt * lax.rsqrt(var + LN_EPS) * g_ref[...] + b_ref[...]


def _out_projection(a, proj, x, gate, w_bf16, ln_g, ln_b, alpha):
    b, s, d = x.shape
    tm = min(s, 512)
    row = lambda bi, i: (bi, i, 0)
    return pl.pallas_call(
        functools.partial(_outproj_kernel, alpha),
        grid=(b, s // tm),
        in_specs=[
            pl.BlockSpec((1, tm, d), row),
            pl.BlockSpec((1, tm, d), lambda bi, i: (bi, i, 3)),
            pl.BlockSpec((1, tm, d), row),
            pl.BlockSpec((1, 1, d), lambda bi, i: (bi, 0, 0)),
            pl.BlockSpec((d, d), lambda bi, i: (0, 0)),
            pl.BlockSpec((1, d), lambda bi, i: (0, 0)),
            pl.BlockSpec((1, d), lambda bi, i: (0, 0)),
        ],
        out_specs=pl.BlockSpec((1, tm, d), row),
        out_shape=jax.ShapeDtypeStruct((b, s, d), F32),
        compiler_params=_cparams(("arbitrary", "arbitrary")),
        name="out_projection",
    )(a, proj, x, gate, w_bf16, ln_g.reshape(1, d), ln_b.reshape(1, d))


def _col_layout():
    q_cols = np.arange(GRID_W).reshape(N_COL_BLOCKS, Q_BLOCK_W)
    q_start = np.clip(q_cols - KW // 2, 0, GRID_W - KW)
    blk_start = np.clip(np.arange(N_COL_BLOCKS) * Q_BLOCK_W - KW // 2, 0, GRID_W - K_BLOCK_W)
    k_cols = blk_start[:, None] + np.arange(K_BLOCK_W)
    kc = k_cols[:, None, :]
    in_win = (kc >= q_start[:, :, None]) & (kc < q_start[:, :, None] + KW)
    dcol = np.clip(kc - q_cols[:, :, None] + KW - 1, 0, RPB_COLS - 1)
    return blk_start, in_win, dcol


def _window_base(rb, rows):
    return np.clip(rb * Q_ROWS - KH // 2, 0, rows - K_ROWS)


def _na_mask_table(rows):
    _, in_win, _ = _col_layout()
    n_rb = rows // Q_ROWS
    out = np.zeros((3, N_COL_BLOCKS, Q_ROWS, K_ROWS // ROWS_PER_LANE_TILE, Q_BLOCK_W, LANES), np.float32)
    for cls, rb in enumerate((0, 1, n_rb - 1)):
        kr0 = _window_base(rb, rows)
        for qr in range(Q_ROWS):
            r = rb * Q_ROWS + qr
            r0 = np.clip(r - KH // 2, 0, rows - KH)
            for kr in range(K_ROWS):
                row_ok = r0 <= kr0 + kr < r0 + KH
                g, j = divmod(kr, ROWS_PER_LANE_TILE)
                ok = in_win & row_ok
                out[cls, :, qr, g, :, j * K_BLOCK_W:(j + 1) * K_BLOCK_W] = np.where(ok, 0.0, NEG_INF)
    return out


def _rpb_expand_kernel(r_ref, e_ref, o_ref):
    acc = None
    for j in range(ROWS_PER_LANE_TILE):
        r = r_ref[j]
        hi = r.astype(BF16)
        r1 = r - hi.astype(F32)
        mid = r1.astype(BF16)
        lo = (r1 - mid.astype(F32)).astype(BF16)
        e = e_ref[0, j]
        part = _dot(hi, e) + _dot(mid, e) + _dot(lo, e)
        acc = part if acc is None else acc + part
    o_ref[0] = acc * LOG2E


def _na_bias_table(rpb):
    h = rpb.shape[0]
    _, _, dcol = _col_layout()
    kpad = 32
    nj = ROWS_PER_LANE_TILE
    sel = dcol[:, None, :, :] == np.arange(kpad)[None, :, None, None]
    onehot = np.zeros((N_COL_BLOCKS, nj, kpad, Q_BLOCK_W, nj, K_BLOCK_W), np.float32)
    for j in range(nj):
        onehot[:, j, :, :, j, :] = sel
    onehot = onehot.reshape(N_COL_BLOCKS, nj, kpad, Q_BLOCK_W * LANES)
    hi_pad = N_DSTART + nj - 1 - DSTART_SHIFT - RPB_ROWS
    rp = jnp.pad(rpb, ((0, 0), (DSTART_SHIFT, hi_pad), (0, kpad - RPB_COLS)))
    shifted = jnp.stack([rp[:, j:j + N_DSTART] for j in range(nj)], axis=0).reshape(nj, h * N_DSTART, kpad)
    t = pl.pallas_call(
        _rpb_expand_kernel,
        grid=(N_COL_BLOCKS,),
        in_specs=[
            pl.BlockSpec(shifted.shape, lambda n: (0, 0, 0)),
            pl.BlockSpec((1, nj, kpad, Q_BLOCK_W * LANES), lambda n: (n, 0, 0, 0)),
        ],
        out_specs=pl.BlockSpec((1, h * N_DSTART, Q_BLOCK_W * LANES), lambda n: (n, 0, 0)),
        out_shape=jax.ShapeDtypeStruct((N_COL_BLOCKS, h * N_DSTART, Q_BLOCK_W * LANES), F32),
        compiler_params=_cparams(("arbitrary",)),
        name="rpb_expand",
    )(shifted, jnp.asarray(onehot, BF16))
    return t.reshape(N_COL_BLOCKS, h // HEADS_PER_TILE, HEADS_PER_TILE, N_DSTART, Q_BLOCK_W, LANES)


def _na_kernel(rows, blk_start, q_ref, k_ref, v_ref, kc_ref, vc_ref, t_ref, m_ref, o_ref):
    rb = pl.program_id(2)
    n_rb = pl.num_programs(2)
    kr0 = jnp.clip(rb * Q_ROWS - KH // 2, 0, rows - K_ROWS)
    cls = jnp.where(rb == 0, 0, jnp.where(rb == n_rb - 1, 2, 1))
    off = kr0 - rb * Q_ROWS + KH - 1 + DSTART_SHIFT
    lane = lax.broadcasted_iota(jnp.int32, (1, LANES), 1)
    kctx = kc_ref[0].astype(BF16)
    vctx = vc_ref[0].astype(BF16)
    n_groups = K_ROWS // ROWS_PER_LANE_TILE
    nq = Q_ROWS * Q_BLOCK_W

    def queries(n):
        qn = jnp.concatenate(
            [q_ref[0, qr * GRID_W + n * Q_BLOCK_W: qr * GRID_W + (n + 1) * Q_BLOCK_W, :] for qr in range(Q_ROWS)],
            axis=0) * (HEAD_DIM ** -0.5 * LOG2E)
        return jnp.concatenate(
            [jnp.where((lane >= h * HEAD_DIM) & (lane < (h + 1) * HEAD_DIM), qn, 0.0) for h in range(HEADS_PER_TILE)],
            axis=0).astype(BF16)

    q2s = [queries(n) for n in range(N_COL_BLOCKS)]
    s_ctx_all = _dot_nt(jnp.concatenate(q2s, axis=0), kctx)

    def scores(n):
        c0 = int(blk_start[n])
        q2 = q2s[n]
        window = lambda ref: jnp.concatenate(
            [ref[0, pl.ds(pl.multiple_of((kr0 + kr) * GRID_W + c0, SUBLANES), K_BLOCK_W), :] for kr in range(K_ROWS)],
            axis=0).astype(BF16)
        bias = jnp.concatenate([
            jnp.concatenate([t_ref[n, 0, h, off + ROWS_PER_LANE_TILE * g - qr] + m_ref[cls, n, qr, g]
                             for g in range(n_groups)], axis=1)
            for h in range(HEADS_PER_TILE) for qr in range(Q_ROWS)], axis=0)
        rows_n = HEADS_PER_TILE * nq
        return _dot_nt(q2, window(k_ref)) + bias, s_ctx_all[n * rows_n:(n + 1) * rows_n], window(v_ref)

    def finish(n, s_lat, s_ctx, vn):
        m = jnp.maximum(jnp.max(s_lat, axis=-1, keepdims=True), jnp.max(s_ctx, axis=-1, keepdims=True))
        e_lat = jnp.exp2(s_lat - m)
        e_ctx = jnp.exp2(s_ctx - m)
        den = jnp.sum(e_lat, axis=-1, keepdims=True) + jnp.sum(e_ctx, axis=-1, keepdims=True)
        o2 = (_dot(e_lat.astype(BF16), vn) + _dot(e_ctx.astype(BF16), vctx)) / den
        o_n = o2[0:nq]
        for h in range(1, HEADS_PER_TILE):
            o_n = jnp.where((lane >= h * HEAD_DIM) & (lane < (h + 1) * HEAD_DIM), o2[h * nq:(h + 1) * nq], o_n)
        for qr in range(Q_ROWS):
            o_ref[0, qr * GRID_W + n * Q_BLOCK_W: qr * GRID_W + (n + 1) * Q_BLOCK_W, :] = (
                o_n[qr * Q_BLOCK_W:(qr + 1) * Q_BLOCK_W].astype(o_ref.dtype))

    cur = scores(0)
    for n in range(N_COL_BLOCKS):
        nxt = scores(n + 1) if n + 1 < N_COL_BLOCKS else None
        finish(n, *cur)
        cur = nxt


def _neighbourhood_attention(proj, proj_c, bias_table, mask_table):
    b, s, d4 = proj.shape
    d = d4 // 4
    c = proj_c.shape[1]
    rows = s // GRID_W
    n_tiles = d // LANES
    blk_start, _, _ = _col_layout()
    tq = Q_ROWS * GRID_W
    return pl.pallas_call(
        functools.partial(_na_kernel, rows, blk_start),
        grid=(b, n_tiles, rows // Q_ROWS),
        in_specs=[
            pl.BlockSpec((1, tq, LANES), lambda bi, hp, rb: (bi, rb, hp)),
            pl.BlockSpec((1, s, LANES), lambda bi, hp, rb: (bi, 0, n_tiles + hp)),
            pl.BlockSpec((1, s, LANES), lambda bi, hp, rb: (bi, 0, 2 * n_tiles + hp)),
            pl.BlockSpec((1, c, LANES), lambda bi, hp, rb: (bi, 0, n_tiles + hp)),
            pl.BlockSpec((1, c, LANES), lambda bi, hp, rb: (bi, 0, 2 * n_tiles + hp)),
            pl.BlockSpec(bias_table.shape[:1] + (1,) + bias_table.shape[2:], lambda bi, hp, rb: (0, hp, 0, 0, 0, 0)),
            pl.BlockSpec(mask_table.shape, lambda bi, hp, rb: (0, 0, 0, 0, 0, 0)),
        ],
        out_specs=pl.BlockSpec((1, tq, LANES), lambda bi, hp, rb: (bi, rb, hp)),
        out_shape=jax.ShapeDtypeStruct((b, s, d), BF16),
        compiler_params=_cparams(("arbitrary", "arbitrary", "arbitrary")),
        name="neighbourhood_attention",
    )(proj, proj, proj, proj_c, proj_c, bias_table, mask_table)


def _ctx_attn_kernel(q_ref, k_ref, v_ref, o_ref):
    lane = lax.broadcasted_iota(jnp.int32, (1, LANES), 1)
    q = q_ref[0] * (HEAD_DIM ** -0.5 * LOG2E)
    k = k_ref[0].astype(BF16)
    v = v_ref[0].astype(BF16)
    out = None
    for h in range(HEADS_PER_TILE):
        in_head = (lane >= h * HEAD_DIM) & (lane < (h + 1) * HEAD_DIM)
        s = _dot_nt(jnp.where(in_head, q, 0.0).astype(BF16), k)
        e = jnp.exp2(s - jnp.max(s, axis=-1, keepdims=True))
        o_h = _dot(e.astype(BF16), v) / jnp.sum(e, axis=-1, keepdims=True)
        out = o_h if out is None else jnp.where(in_head, o_h, out)
    o_ref[0] = out.astype(o_ref.dtype)


def _context_attention(proj_c):
    b, c, d4 = proj_c.shape
    d = d4 // 4
    n_tiles = d // LANES
    return pl.pallas_call(
        _ctx_attn_kernel,
        grid=(b, n_tiles),
        in_specs=[
            pl.BlockSpec((1, c, LANES), lambda bi, hp: (bi, 0, hp)),
            pl.BlockSpec((1, c, LANES), lambda bi, hp: (bi, 0, n_tiles + hp)),
            pl.BlockSpec((1, c, LANES), lambda bi, hp: (bi, 0, 2 * n_tiles + hp)),
        ],
        out_specs=pl.BlockSpec((1, c, LANES), lambda bi, hp: (bi, 0, hp)),
        out_shape=jax.ShapeDtypeStruct((b, c, d), BF16),
        compiler_params=_cparams(("arbitrary", "arbitrary")),
        name="context_attention",
    )(proj_c, proj_c, proj_c)


def _conv_block(length):
    return min(CONV_BLOCK, length)


def _dft_matrices(p):
    k = np.arange(p, dtype=np.float64)[:, None]
    n = np.arange(p, dtype=np.float64)[None, :]
    ang = 2.0 * np.pi * k * n / (2 * p)
    f_im = -np.sin(ang)
    f_im[0] = np.cos(np.pi * n[0])
    fwd = np.concatenate([np.cos(ang), f_im], axis=0)
    wk = np.where(k == 0, 1.0, 2.0) / (2 * p)
    ang_i = 2.0 * np.pi * k * (n + p) / (2 * p)
    g_im = -np.sin(ang_i) * wk
    g_im[0] = np.cos(np.pi * (n[0] + p)) / (2 * p)
    inv = np.concatenate([np.cos(ang_i) * wk, g_im], axis=0).T
    return fwd.astype(np.float32), inv.astype(np.float32)


def _filter_kernel(length, p, z_ref, w1_ref, b1_ref, w2_ref, b2_ref, w3_ref, b3_ref, fr_ref, w4_ref,
                   dl_ref, f_ref, o_ref, a_scr, t_scr, prev_scr, norm_scr):
    nb = length // p
    di = pl.program_id(2)
    cb = o_ref.shape[-1]

    @pl.when((pl.program_id(0) == 0) & (pl.program_id(1) == 0) & (di == 0))
    def _():
        a = jnp.sin(fr_ref[0:1, :] * (_dot3(z_ref[...], w1_ref[...]) + b1_ref[...]))
        a = jnp.sin(fr_ref[1:2, :] * (_dot3(a, w2_ref[...]) + b2_ref[...]))
        a_scr[...] = jnp.sin(fr_ref[2:3, :] * (_dot3(a, w3_ref[...]) + b3_ref[...]))

    def block_dft(blk):
        start = pl.multiple_of(blk * p, p)
        return _dot(f_ref[...], t_scr[pl.ds(start, p), :].astype(BF16))

    @pl.when(di == 0)
    def _():
        total = jnp.zeros((1, cb), F32)
        for blk in range(2 * nb):
            back = blk < nb
            idx = lax.broadcasted_iota(jnp.int32, (p, cb), 0) + (blk % nb) * p
            t = ((length - idx) if back else idx).astype(F32) * (1.0 / (length - 1))
            taps = _dot3(a_scr[blk * p:(blk + 1) * p, :], w4_ref[0, 1 if back else 0]) * jnp.exp(-t * dl_ref[...])
            if blk == 0:
                taps = jnp.where(idx == 0, 0.0, taps)
            t_scr[blk * p:(blk + 1) * p, :] = taps
            total = total + jnp.sum(jnp.abs(taps), axis=0, keepdims=True)
        norm_scr[...] = jnp.broadcast_to(1.0 / total, norm_scr.shape)
        prev_scr[...] = block_dft(0)

    nxt = block_dft(di + 1)
    krow = lax.broadcasted_iota(jnp.int32, (2 * p, cb), 0)
    o_ref[0, 0] = (prev_scr[...] + jnp.where((krow & 1) == 0, nxt, -nxt)) * norm_scr[0:1, :]
    prev_scr[...] = nxt


def _hyena_filters(length, w1, b1, w2, b2, w3, b3, w4, freq):
    p = _conv_block(length)
    nb = length // p
    width = w4.shape[-1] // 4
    cb = min(FILTER_LANES, width)
    t = jnp.linspace(0.0, 1.0, length, dtype=F32)[:, None]
    bands = (HY_EMB_DIM - 1) // 2
    wpos = 2.0 * math.pi * jnp.arange(length, dtype=F32)[:, None] / length
    f = jnp.linspace(1e-4, bands - 1, bands, dtype=F32)[None, :]
    z = jnp.concatenate([t, jnp.cos(f * wpos), -jnp.sin(f * wpos)], axis=-1)
    z_back = jnp.concatenate([jnp.zeros((1, HY_EMB_DIM), F32), z[1:][::-1]], axis=0)
    emb_pad = 64
    z_all = jnp.pad(jnp.concatenate([z_back, z], axis=0), ((0, 0), (0, emb_pad - HY_EMB_DIM)))
    w1p = jnp.pad(w1, ((0, emb_pad - HY_EMB_DIM), (0, 0)))
    order = w4.shape[-1] // (2 * width)
    w4r = jnp.transpose(w4.reshape(w4.shape[0], order, 2, width), (1, 2, 0, 3))
    max_decay = math.log(HY_TARGET) / HY_FAST_DECAY
    min_decay = math.log(HY_TARGET) / HY_SLOW_DECAY
    deltas = jnp.abs(jnp.linspace(min_decay, max_decay, width, dtype=F32)).reshape(1, width)
    fwd = jnp.asarray(_dft_matrices(p)[0]).astype(BF16)
    hid = w2.shape[0]
    full = lambda shape: pl.BlockSpec(shape, lambda o, c, di: (0,) * len(shape))
    return pl.pallas_call(
        functools.partial(_filter_kernel, length, p),
        grid=(order, width // cb, 2 * nb - 1),
        in_specs=[
            full(z_all.shape), full(w1p.shape), full((1, hid)), full(w2.shape), full((1, hid)),
            full(w3.shape), full((1, hid)), full(freq.shape),
            pl.BlockSpec((1, 2, hid, cb), lambda o, c, di: (o, 0, 0, c)),
            pl.BlockSpec((1, cb), lambda o, c, di: (0, c)),
            full(fwd.shape),
        ],
        out_specs=pl.BlockSpec((1, 1, 2 * p, cb), lambda o, c, di: (o, di, 0, c)),
        out_shape=jax.ShapeDtypeStruct((order, 2 * nb - 1, 2 * p, width), F32),
        scratch_shapes=[pltpu.VMEM((2 * length, hid), F32), pltpu.VMEM((2 * length, cb), F32),
                        pltpu.VMEM((2 * p, cb), F32), pltpu.VMEM((SUBLANES, cb), F32)],
        compiler_params=_cparams(("arbitrary", "arbitrary", "arbitrary")),
        name="hyena_filters",
    )(z_all, w1p, b1.reshape(1, hid), w2, b2.reshape(1, hid), w3, b3.reshape(1, hid), freq, w4r, deltas, fwd)


def _short_conv_block(pad_ref, j, p, w_ref, b_ref):
    base = SUBLANES + j * p
    return (pad_ref[base - 1: base - 1 + p, :] * w_ref[0:1, :] + pad_ref[base: base + p, :] * w_ref[1:2, :]
            + pad_ref[base + 1: base + 1 + p, :] * w_ref[2:3, :] + b_ref[...])


def _fill_padded(pad_ref, src_ref, length):
    zeros = jnp.zeros((SUBLANES, pad_ref.shape[1]), F32)
    pad_ref[0:SUBLANES, :] = zeros
    pad_ref[SUBLANES + length: 2 * SUBLANES + length, :] = zeros
    pad_ref[SUBLANES: SUBLANES + length, :] = src_ref[...].astype(F32)


def _conv_kernel(length, p, conv_u, u_ref, m_ref, h_ref, f_ref, g_ref, cwu_ref, cbu_ref, cwm_ref, cbm_ref,
                 skip_ref, o_ref, pad, uconv, uhat, yhat):
    nb = length // p
    nbat = u_ref.shape[0]

    if conv_u:
        for bi in range(nbat):
            _fill_padded(pad.at[bi], u_ref.at[bi], length)
    fwd = f_ref[...]
    for j in range(nb):
        blocks = []
        for bi in range(nbat):
            if conv_u:
                uj = _short_conv_block(pad.at[bi], j, p, cwu_ref, cbu_ref)
                uconv[bi, j * p:(j + 1) * p, :] = uj
            else:
                uj = u_ref[bi, j * p:(j + 1) * p, :]
            blocks.append(uj.astype(BF16))
        uhat[j] = _dot(fwd, jnp.concatenate(blocks, axis=1))
    for bi in range(nbat):
        _fill_padded(pad.at[bi], m_ref.at[bi], length)

    inv = g_ref[...]
    first = lax.broadcasted_iota(jnp.int32, (SUBLANES, nbat * LANES), 0) == 0
    for i in range(nb):
        for c in range(p // ACC_ROWS):
            r0 = c * ACC_ROWS
            acc = [[jnp.zeros((ACC_ROWS, LANES), F32) for _ in range(2)] for _ in range(nbat)]
            for j in range(nb):
                di = i - j + nb - 1
                ha = h_ref[0, di, r0:r0 + ACC_ROWS, :]
                hb = h_ref[0, di, p + r0:p + r0 + ACC_ROWS, :]
                for bi in range(nbat):
                    ua = uhat[j, r0:r0 + ACC_ROWS, bi * LANES:(bi + 1) * LANES]
                    ub = uhat[j, p + r0:p + r0 + ACC_ROWS, bi * LANES:(bi + 1) * LANES]
                    acc[bi][0] = acc[bi][0] + (ha * ua - hb * ub)
                    acc[bi][1] = acc[bi][1] + (ha * ub + hb * ua)
            for bi in range(nbat):
                yhat[r0:r0 + ACC_ROWS, bi * LANES:(bi + 1) * LANES] = acc[bi][0]
                yhat[p + r0:p + r0 + ACC_ROWS, bi * LANES:(bi + 1) * LANES] = acc[bi][1]
        dc = jnp.zeros((SUBLANES, nbat * LANES), F32)
        ny = jnp.zeros((SUBLANES, nbat * LANES), F32)
        for j in range(nb):
            di = i - j + nb - 1
            dc = dc + jnp.concatenate([h_ref[0, di, 0:SUBLANES, :]] * nbat, axis=1) * uhat[j, 0:SUBLANES, :]
            ny = ny + jnp.concatenate([h_ref[0, di, p:p + SUBLANES, :]] * nbat, axis=1) * uhat[j, p:p + SUBLANES, :]
        yhat[0:SUBLANES, :] = jnp.where(first, dc, yhat[0:SUBLANES, :])
        yhat[p:p + SUBLANES, :] = jnp.where(first, ny, yhat[p:p + SUBLANES, :])

        y = _dot(inv, yhat[...].astype(BF16))
        for bi in range(nbat):
            mi = _short_conv_block(pad.at[bi], i, p, cwm_ref, cbm_ref)
            src = uconv if conv_u else u_ref
            yi = y[:, bi * LANES:(bi + 1) * LANES] + src[bi, i * p:(i + 1) * p, :].astype(F32) * skip_ref[...]
            o_ref[bi, i * p:(i + 1) * p, :] = (mi * yi).astype(o_ref.dtype)


def _long_conv_gate(u_src, u_col, conv_u, proj, m_col, spectra, order, conv_w, conv_b, skip, out_dtype):
    b, length, _ = proj.shape
    width = skip.shape[-1]
    p = _conv_block(length)
    nb = length // p
    cb = LANES
    ncb = width // cb
    nbat = CONV_BATCH
    assert b % nbat == 0 and p % ACC_ROWS == 0
    fwd_np, inv_np = _dft_matrices(p)
    fwd = jnp.asarray(fwd_np).astype(BF16)
    inv = jnp.asarray(inv_np).astype(BF16)
    col = lambda base: (lambda c, bi: (bi, 0, base * ncb + c))
    par = lambda base: (lambda c, bi: (0, base * ncb + c))
    once = pl.Buffered(1)
    return pl.pallas_call(
        functools.partial(_conv_kernel, length, p, conv_u),
        grid=(ncb, b // nbat),
        in_specs=[
            pl.BlockSpec((nbat, length, cb), col(u_col)),
            pl.BlockSpec((nbat, length, cb), col(m_col)),
            pl.BlockSpec((1, 2 * nb - 1, 2 * p, cb), lambda c, bi: (order, 0, 0, c), pipeline_mode=once),
            pl.BlockSpec(fwd.shape, lambda c, bi: (0, 0), pipeline_mode=once),
            pl.BlockSpec(inv.shape, lambda c, bi: (0, 0), pipeline_mode=once),
            pl.BlockSpec((3, cb), par(0)),
            pl.BlockSpec((1, cb), par(0)),
            pl.BlockSpec((3, cb), par(m_col)),
            pl.BlockSpec((1, cb), par(m_col)),
            pl.BlockSpec((1, cb), lambda c, bi: (0, c)),
        ],
        out_specs=pl.BlockSpec((nbat, length, cb), lambda c, bi: (bi, 0, c)),
        out_shape=jax.ShapeDtypeStruct((b, length, width), out_dtype),
        scratch_shapes=[
            pltpu.VMEM((nbat, length + 2 * SUBLANES, cb), F32),
            pltpu.VMEM((nbat, length, cb) if conv_u else (nbat, SUBLANES, cb), F32),
            pltpu.VMEM((nb, 2 * p, nbat * cb), F32), pltpu.VMEM((2 * p, nbat * cb), F32),
        ],
        compiler_params=_cparams(("arbitrary", "arbitrary")),
        name="long_conv_gate",
    )(u_src, proj, spectra, fwd, inv, conv_w, conv_b.reshape(1, -1), conv_w, conv_b.reshape(1, -1),
      skip[order].reshape(1, width))


def _hyena_core(proj, spectra, conv_w, conv_b, skip):
    z = _long_conv_gate(proj, 0, True, proj, 1, spectra, 0, conv_w, conv_b, skip, F32)
    return _long_conv_gate(z, 0, False, proj, 2, spectra, 1, conv_w, conv_b, skip, BF16)


def kernel(x, c, ctx, c_ctx, w_ada, b_ada, w_in, w_out, ln_g, ln_b, na_rpb, hy_conv_w, hy_conv_b, hy_f_w1, hy_f_b1,
           hy_f_w2, hy_f_b2, hy_f_w3, hy_f_b3, hy_f_w4, hy_f_freq, hy_skip):
    depth = w_in.shape[0]
    b, s, d = x.shape
    n_ctx = ctx.shape[1]
    n_mixers = 2
    alpha = (2 * depth) ** 0.25

    cc = jnp.concatenate([c, c_ctx[None, :], jnp.zeros((2 * SUBLANES - b - 1, d), F32)], axis=0)
    mod = _ada_vectors(cc, w_ada, b_ada)
    w_in_bf = w_in.astype(BF16)
    w_out_bf = w_out.astype(BF16)
    mask_table = jnp.asarray(_na_mask_table(s // GRID_W))

    for i in range(depth):
        last = i == depth - 1
        j = i // n_mixers
        shift, scale, gate = [mod[i, :b, None, k * d:(k + 1) * d] for k in range(3)]
        shift_c, scale_c, gate_c = [jnp.broadcast_to(mod[i, b, k * d:(k + 1) * d], (b, 1, d)) for k in range(3)]
        proj_dtype = F32 if i % n_mixers == 0 else BF16
        proj = _in_projection(x, scale, shift, w_in_bf[i], proj_dtype)
        proj_c = None
        if i % n_mixers == 0 or not last:
            proj_c = _in_projection(ctx, scale_c, shift_c, w_in_bf[i], proj_dtype)
        if i % n_mixers == 0:
            bias_table = _na_bias_table(na_rpb[j])
            a = _neighbourhood_attention(proj, proj_c, bias_table, mask_table)
            a_c = None if last else _context_attention(proj_c)
        else:
            filt = (hy_f_w1[j], hy_f_b1[j], hy_f_w2[j], hy_f_b2[j], hy_f_w3[j], hy_f_b3[j], hy_f_w4[j], hy_f_freq[j])
            a = _hyena_core(proj, _hyena_filters(s, *filt), hy_conv_w[j], hy_conv_b[j], hy_skip[j])
            a_c = None if last else _hyena_core(proj_c, _hyena_filters(n_ctx, *filt), hy_conv_w[j], hy_conv_b[j],
                                                hy_skip[j])
        x = _out_projection(a, proj, x, gate, w_out_bf[i], ln_g[i], ln_b[i], alpha)
        if not last:
            ctx = _out_projection(a_c, proj_c, ctx, gate_c, w_out_bf[i], ln_g[i], ln_b[i], alpha)
    return x
```

```python
import functools
import math

import numpy as np
import jax
import jax.numpy as jnp
from jax import lax
from jax.experimental import pallas as pl
from jax.experimental.pallas import tpu as pltpu

F32 = jnp.float32
BF16 = jnp.bfloat16

HEAD_DIM = 64
GRID_W = 64
KH = 8
KW = 16
Q_BLOCK_W = 16
K_BLOCK_W = Q_BLOCK_W + KW
N_COL_BLOCKS = GRID_W // Q_BLOCK_W
RPB_ROWS = 2 * KH - 1
RPB_COLS = 2 * KW - 1
HY_EMB_DIM = 33
HY_FAST_DECAY = 0.3
HY_SLOW_DECAY = 1.5
HY_TARGET = 1e-2
LN_EPS = 1e-5
NEG_INF = -1e30
LOG2E = 1.4426950408889634

LANES = 128
SUBLANES = 8
HEADS_PER_TILE = LANES // HEAD_DIM
VMEM_LIMIT = 56 * 1024 * 1024

Q_ROWS = 8
NA_BLOCKS_PER_STEP = 2
K_ROWS = 16
ROWS_PER_LANE_TILE = LANES // K_BLOCK_W
N_DSTART = 28
DSTART_SHIFT = 8
CONV_BLOCK = 1024
CONV_BATCH = 2
ACC_ROWS = 32
FILTER_LANES = 256
IN_PROJ_ROWS = 512


def _cparams(sem):
    return pltpu.CompilerParams(dimension_semantics=sem, vmem_limit_bytes=VMEM_LIMIT)


def _split_bf16(a):
    hi = a.astype(BF16)
    lo = (a - hi.astype(F32)).astype(BF16)
    return hi, lo


def _dot(a, b):
    return jnp.dot(a, b, preferred_element_type=F32)


def _dot_nt(a, b):
    return lax.dot_general(a, b, (((1,), (1,)), ((), ())), preferred_element_type=F32)


def _dot3(a, b):
    ah, al = _split_bf16(a)
    bh, bl = _split_bf16(b)
    return _dot(ah, bh) + _dot(al, bh) + _dot(ah, bl)


def _silu(x):
    return x * jax.nn.sigmoid(x)


def _ada_kernel(cc_ref, w_ref, b_ref, o_ref):
    o_ref[0] = _dot3(_silu(cc_ref[...]), w_ref[0]) + b_ref[0]


def _ada_vectors(cc, w_ada, b_ada):
    depth, d, n = w_ada.shape
    tn = min(n, 1024)
    return pl.pallas_call(
        _ada_kernel,
        grid=(depth, n // tn),
        in_specs=[
            pl.BlockSpec(cc.shape, lambda i, j: (0, 0)),
            pl.BlockSpec((1, d, tn), lambda i, j: (i, 0, j)),
            pl.BlockSpec((1, 1, tn), lambda i, j: (i, 0, j)),
        ],
        out_specs=pl.BlockSpec((1, cc.shape[0], tn), lambda i, j: (i, 0, j)),
        out_shape=jax.ShapeDtypeStruct((depth, cc.shape[0], n), F32),
        compiler_params=_cparams(("arbitrary", "arbitrary")),
        name="ada_vectors",
    )(cc, w_ada, b_ada.reshape(depth, 1, n))


def _inproj_kernel(x_ref, sc_ref, sh_ref, w_ref, o_ref):
    h = (x_ref[0] * (1.0 + sc_ref[0]) + sh_ref[0]).astype(BF16)
    o_ref[0] = _dot(h, w_ref[...]).astype(o_ref.dtype)


def _in_projection(x, scale, shift, w_bf16, out_dtype):
    b, s, d = x.shape
    n = w_bf16.shape[1]
    tm = min(s, IN_PROJ_ROWS)
    return pl.pallas_call(
        _inproj_kernel,
        grid=(b, s // tm),
        in_specs=[
            pl.BlockSpec((1, tm, d), lambda bi, i: (bi, i, 0)),
            pl.BlockSpec((1, 1, d), lambda bi, i: (bi, 0, 0)),
            pl.BlockSpec((1, 1, d), lambda bi, i: (bi, 0, 0)),
            pl.BlockSpec((d, n), lambda bi, i: (0, 0), pipeline_mode=pl.Buffered(1)),
        ],
        out_specs=pl.BlockSpec((1, tm, n), lambda bi, i: (bi, i, 0)),
        out_shape=jax.ShapeDtypeStruct((b, s, n), out_dtype),
        compiler_params=_cparams(("arbitrary", "arbitrary")),
        name="in_projection",
    )(x, scale, shift, w_bf16)


def _outproj_kernel(alpha, a_ref, z_ref, x_ref, gate_ref, w_ref, g_ref, b_ref, o_ref):
    a = a_ref[0].astype(F32) * _silu(z_ref[0].astype(F32))
    y = _dot(a.astype(BF16), w_ref[...])
    r = alpha * x_ref[0] + gate_ref[0] * y
    mu = jnp.mean(r, axis=-1, keepdims=True)
    dlt = r - mu
    var = jnp.mean(dlt * dlt, axis=-1, keepdims=True)
    o_ref[0] = dlt * lax.rsqrt(var + LN_EPS) * g_ref[...] + b_ref[...]


def _out_projection(a, proj, x, gate, w_bf16, ln_g, ln_b, alpha):
    b, s, d = x.shape
    tm = min(s, 512)
    row = lambda bi, i: (bi, i, 0)
    return pl.pallas_call(
        functools.partial(_outproj_kernel, alpha),
        grid=(b, s // tm),
        in_specs=[
            pl.BlockSpec((1, tm, d), row),
            pl.BlockSpec((1, tm, d), lambda bi, i: (bi, i, 3)),
            pl.BlockSpec((1, tm, d), row),
            pl.BlockSpec((1, 1, d), lambda bi, i: (bi, 0, 0)),
            pl.BlockSpec((d, d), lambda bi, i: (0, 0)),
            pl.BlockSpec((1, d), lambda bi, i: (0, 0)),
            pl.BlockSpec((1, d), lambda bi, i: (0, 0)),
        ],
        out_specs=pl.BlockSpec((1, tm, d), row),
        out_shape=jax.ShapeDtypeStruct((b, s, d), F32),
        compiler_params=_cparams(("arbitrary", "arbitrary")),
        name="out_projection",
    )(a, proj, x, gate, w_bf16, ln_g.reshape(1, d), ln_b.reshape(1, d))


def _col_layout():
    q_cols = np.arange(GRID_W).reshape(N_COL_BLOCKS, Q_BLOCK_W)
    q_start = np.clip(q_cols - KW // 2, 0, GRID_W - KW)
    blk_start = np.clip(np.arange(N_COL_BLOCKS) * Q_BLOCK_W - KW // 2, 0, GRID_W - K_BLOCK_W)
    k_cols = blk_start[:, None] + np.arange(K_BLOCK_W)
    kc = k_cols[:, None, :]
    in_win = (kc >= q_start[:, :, None]) & (kc < q_start[:, :, None] + KW)
    dcol = np.clip(kc - q_cols[:, :, None] + KW - 1, 0, RPB_COLS - 1)
    return blk_start, in_win, dcol


def _window_base(rb, rows):
    return np.clip(rb * Q_ROWS - KH // 2, 0, rows - K_ROWS)


def _na_mask_table(rows):
    _, in_win, _ = _col_layout()
    n_rb = rows // Q_ROWS
    out = np.zeros((3, N_COL_BLOCKS, Q_ROWS, K_ROWS // ROWS_PER_LANE_TILE, Q_BLOCK_W, LANES), np.float32)
    for cls, rb in enumerate((0, 1, n_rb - 1)):
        kr0 = _window_base(rb, rows)
        for qr in range(Q_ROWS):
            r = rb * Q_ROWS + qr
            r0 = np.clip(r - KH // 2, 0, rows - KH)
            for kr in range(K_ROWS):
                row_ok = r0 <= kr0 + kr < r0 + KH
                g, j = divmod(kr, ROWS_PER_LANE_TILE)
                ok = in_win & row_ok
                out[cls, :, qr, g, :, j * K_BLOCK_W:(j + 1) * K_BLOCK_W] = np.where(ok, 0.0, NEG_INF)
    return out


def _rpb_expand_kernel(r_ref, e_ref, o_ref):
    acc = None
    for j in range(ROWS_PER_LANE_TILE):
        r = r_ref[j]
        hi = r.astype(BF16)
        r1 = r - hi.astype(F32)
        mid = r1.astype(BF16)
        lo = (r1 - mid.astype(F32)).astype(BF16)
        e = e_ref[0, j]
        part = _dot(hi, e) + _dot(mid, e) + _dot(lo, e)
        acc = part if acc is None else acc + part
    o_ref[0] = acc * LOG2E


def _na_bias_table(rpb):
    h = rpb.shape[0]
    _, _, dcol = _col_layout()
    kpad = 32
    nj = ROWS_PER_LANE_TILE
    sel = dcol[:, None, :, :] == np.arange(kpad)[None, :, None, None]
    onehot = np.zeros((N_COL_BLOCKS, nj, kpad, Q_BLOCK_W, nj, K_BLOCK_W), np.float32)
    for j in range(nj):
        onehot[:, j, :, :, j, :] = sel
    onehot = onehot.reshape(N_COL_BLOCKS, nj, kpad, Q_BLOCK_W * LANES)
    hi_pad = N_DSTART + nj - 1 - DSTART_SHIFT - RPB_ROWS
    rp = jnp.pad(rpb, ((0, 0), (DSTART_SHIFT, hi_pad), (0, kpad - RPB_COLS)))
    shifted = jnp.stack([rp[:, j:j + N_DSTART] for j in range(nj)], axis=0).reshape(nj, h * N_DSTART, kpad)
    t = pl.pallas_call(
        _rpb_expand_kernel,
        grid=(N_COL_BLOCKS,),
        in_specs=[
            pl.BlockSpec(shifted.shape, lambda n: (0, 0, 0)),
            pl.BlockSpec((1, nj, kpad, Q_BLOCK_W * LANES), lambda n: (n, 0, 0, 0)),
        ],
        out_specs=pl.BlockSpec((1, h * N_DSTART, Q_BLOCK_W * LANES), lambda n: (n, 0, 0)),
        out_shape=jax.ShapeDtypeStruct((N_COL_BLOCKS, h * N_DSTART, Q_BLOCK_W * LANES), F32),
        compiler_params=_cparams(("arbitrary",)),
        name="rpb_expand",
    )(shifted, jnp.asarray(onehot, BF16))
    return t.reshape(N_COL_BLOCKS, h // HEADS_PER_TILE, HEADS_PER_TILE, N_DSTART, Q_BLOCK_W, LANES)


def _na_kernel(rows, blk_start, q_ref, k_ref, v_ref, kc_ref, vc_ref, t_ref, m_ref, o_ref):
    n_rb = rows // Q_ROWS
    lane = lax.broadcasted_iota(jnp.int32, (1, LANES), 1)
    kctx = kc_ref[0].astype(BF16)
    vctx = vc_ref[0].astype(BF16)
    n_groups = K_ROWS // ROWS_PER_LANE_TILE
    nq = Q_ROWS * Q_BLOCK_W
    rows_n = HEADS_PER_TILE * nq

    def prepare(sub):
        rb = pl.program_id(2) * NA_BLOCKS_PER_STEP + sub
        kr0 = jnp.clip(rb * Q_ROWS - KH // 2, 0, rows - K_ROWS)
        cls = jnp.where(rb == 0, 0, jnp.where(rb == n_rb - 1, 2, 1))
        off = kr0 - rb * Q_ROWS + KH - 1 + DSTART_SHIFT
        q0 = sub * Q_ROWS * GRID_W
        q2s = []
        for n in range(N_COL_BLOCKS):
            qn = jnp.concatenate(
                [q_ref[0, q0 + qr * GRID_W + n * Q_BLOCK_W: q0 + qr * GRID_W + (n + 1) * Q_BLOCK_W, :]
                 for qr in range(Q_ROWS)], axis=0) * (HEAD_DIM ** -0.5 * LOG2E)
            q2s.append(jnp.concatenate(
                [jnp.where((lane >= h * HEAD_DIM) & (lane < (h + 1) * HEAD_DIM), qn, 0.0)
                 for h in range(HEADS_PER_TILE)], axis=0).astype(BF16))
        s_ctx_all = _dot_nt(jnp.concatenate(q2s, axis=0), kctx)
        return kr0, cls, off, q0, q2s, s_ctx_all

    def scores(prep, n):
        kr0, cls, off, _, q2s, s_ctx_all = prep
        c0 = int(blk_start[n])
        window = lambda ref: jnp.concatenate(
            [ref[0, pl.ds(pl.multiple_of((kr0 + kr) * GRID_W + c0, SUBLANES), K_BLOCK_W), :] for kr in range(K_ROWS)],
            axis=0).astype(BF16)
        bias = jnp.concatenate([
            jnp.concatenate([t_ref[n, 0, h, off + ROWS_PER_LANE_TILE * g - qr] + m_ref[cls, n, qr, g]
                             for g in range(n_groups)], axis=1)
            for h in range(HEADS_PER_TILE) for qr in range(Q_ROWS)], axis=0)
        return _dot_nt(q2s[n], window(k_ref)) + bias, s_ctx_all[n * rows_n:(n + 1) * rows_n], window(v_ref)

    def finish(q0, n, s_lat, s_ctx, vn):
        m = jnp.maximum(jnp.max(s_lat, axis=-1, keepdims=True), jnp.max(s_ctx, axis=-1, keepdims=True))
        e_lat = jnp.exp2(s_lat - m)
        e_ctx = jnp.exp2(s_ctx - m)
        den = jnp.sum(e_lat, axis=-1, keepdims=True) + jnp.sum(e_ctx, axis=-1, keepdims=True)
        o2 = (_dot(e_lat.astype(BF16), vn) + _dot(e_ctx.astype(BF16), vctx)) / den
        o_n = o2[0:nq]
        for h in range(1, HEADS_PER_TILE):
            o_n = jnp.where((lane >= h * HEAD_DIM) & (lane < (h + 1) * HEAD_DIM), o2[h * nq:(h + 1) * nq], o_n)
        for qr in range(Q_ROWS):
            o_ref[0, q0 + qr * GRID_W + n * Q_BLOCK_W: q0 + qr * GRID_W + (n + 1) * Q_BLOCK_W, :] = (
                o_n[qr * Q_BLOCK_W:(qr + 1) * Q_BLOCK_W].astype(o_ref.dtype))

    order = [(sub, n) for sub in range(NA_BLOCKS_PER_STEP) for n in range(N_COL_BLOCKS)]
    preps = {0: prepare(0)}
    cur = scores(preps[0], 0)
    for idx, (sub, n) in enumerate(order):
        nxt = None
        if idx + 1 < len(order):
            nsub, nn = order[idx + 1]
            if nsub not in preps:
                preps[nsub] = prepare(nsub)
            nxt = scores(preps[nsub], nn)
        finish(preps[sub][3], n, *cur)
        cur = nxt


def _neighbourhood_attention(proj, proj_c, bias_table, mask_table):
    b, s, d4 = proj.shape
    d = d4 // 4
    c = proj_c.shape[1]
    rows = s // GRID_W
    n_tiles = d // LANES
    blk_start, _, _ = _col_layout()
    tq = NA_BLOCKS_PER_STEP * Q_ROWS * GRID_W
    return pl.pallas_call(
        functools.partial(_na_kernel, rows, blk_start),
        grid=(b, n_tiles, s // tq),
        in_specs=[
            pl.BlockSpec((1, tq, LANES), lambda bi, hp, rb: (bi, rb, hp)),
            pl.BlockSpec((1, s, LANES), lambda bi, hp, rb: (bi, 0, n_tiles + hp)),
            pl.BlockSpec((1, s, LANES), lambda bi, hp, rb: (bi, 0, 2 * n_tiles + hp)),
            pl.BlockSpec((1, c, LANES), lambda bi, hp, rb: (bi, 0, n_tiles + hp)),
            pl.BlockSpec((1, c, LANES), lambda bi, hp, rb: (bi, 0, 2 * n_tiles + hp)),
            pl.BlockSpec(bias_table.shape[:1] + (1,) + bias_table.shape[2:], lambda bi, hp, rb: (0, hp, 0, 0, 0, 0)),
            pl.BlockSpec(mask_table.shape, lambda bi, hp, rb: (0, 0, 0, 0, 0, 0)),
        ],
        out_specs=pl.BlockSpec((1, tq, LANES), lambda bi, hp, rb: (bi, rb, hp)),
        out_shape=jax.ShapeDtypeStruct((b, s, d), BF16),
        compiler_params=_cparams(("arbitrary", "arbitrary", "arbitrary")),
        name="neighbourhood_attention",
    )(proj, proj, proj, proj_c, proj_c, bias_table, mask_table)


def _ctx_attn_kernel(q_ref, k_ref, v_ref, o_ref):
    lane = lax.broadcasted_iota(jnp.int32, (1, LANES), 1)
    for t in range(q_ref.shape[-1] // LANES):
        cols = slice(t * LANES, (t + 1) * LANES)
        q = q_ref[0, :, cols] * (HEAD_DIM ** -0.5 * LOG2E)
        k = k_ref[0, :, cols].astype(BF16)
        v = v_ref[0, :, cols].astype(BF16)
        out = None
        for h in range(HEADS_PER_TILE):
            in_head = (lane >= h * HEAD_DIM) & (lane < (h + 1) * HEAD_DIM)
            s = _dot_nt(jnp.where(in_head, q, 0.0).astype(BF16), k)
            e = jnp.exp2(s - jnp.max(s, axis=-1, keepdims=True))
            o_h = _dot(e.astype(BF16), v) / jnp.sum(e, axis=-1, keepdims=True)
            out = o_h if out is None else jnp.where(in_head, o_h, out)
        o_ref[0, :, cols] = out.astype(o_ref.dtype)


def _context_attention(proj_c):
    b, c, d4 = proj_c.shape
    d = d4 // 4
    return pl.pallas_call(
        _ctx_attn_kernel,
        grid=(b,),
        in_specs=[
            pl.BlockSpec((1, c, d), lambda bi: (bi, 0, 0)),
            pl.BlockSpec((1, c, d), lambda bi: (bi, 0, 1)),
            pl.BlockSpec((1, c, d), lambda bi: (bi, 0, 2)),
        ],
        out_specs=pl.BlockSpec((1, c, d), lambda bi: (bi, 0, 0)),
        out_shape=jax.ShapeDtypeStruct((b, c, d), BF16),
        compiler_params=_cparams(("arbitrary",)),
        name="context_attention",
    )(proj_c, proj_c, proj_c)


def _conv_block(length):
    return min(CONV_BLOCK, length)


def _dft_matrices(p):
    k = np.arange(p, dtype=np.float64)[:, None]
    n = np.arange(p, dtype=np.float64)[None, :]
    ang = 2.0 * np.pi * k * n / (2 * p)
    f_im = -np.sin(ang)
    f_im[0] = np.cos(np.pi * n[0])
    fwd = np.concatenate([np.cos(ang), f_im], axis=0)
    wk = np.where(k == 0, 1.0, 2.0) / (2 * p)
    ang_i = 2.0 * np.pi * k * (n + p) / (2 * p)
    g_im = -np.sin(ang_i) * wk
    g_im[0] = np.cos(np.pi * (n[0] + p)) / (2 * p)
    inv = np.concatenate([np.cos(ang_i) * wk, g_im], axis=0).T
    return fwd.astype(np.float32), inv.astype(np.float32)


def _filter_kernel(length, p, z_ref, w1_ref, b1_ref, w2_ref, b2_ref, w3_ref, b3_ref, fr_ref, w4_ref,
                   dl_ref, f_ref, o_ref, a_scr, t_scr, prev_scr, norm_scr):
    nb = length // p
    di = pl.program_id(2)
    cb = o_ref.shape[-1]

    @pl.when((pl.program_id(0) == 0) & (pl.program_id(1) == 0) & (di == 0))
    def _():
        a = jnp.sin(fr_ref[0:1, :] * (_dot3(z_ref[...], w1_ref[...]) + b1_ref[...]))
        a = jnp.sin(fr_ref[1:2, :] * (_dot3(a, w2_ref[...]) + b2_ref[...]))
        a_scr[...] = jnp.sin(fr_ref[2:3, :] * (_dot3(a, w3_ref[...]) + b3_ref[...]))

    def block_dft(blk):
        start = pl.multiple_of(blk * p, p)
        return _dot(f_ref[...], t_scr[pl.ds(start, p), :].astype(BF16))

    @pl.when(di == 0)
    def _():
        total = jnp.zeros((1, cb), F32)
        for blk in range(2 * nb):
            back = blk < nb
            idx = lax.broadcasted_iota(jnp.int32, (p, cb), 0) + (blk % nb) * p
            t = ((length - idx) if back else idx).astype(F32) * (1.0 / (length - 1))
            rows_a = (blk % nb) * p
            taps = _dot3(a_scr[rows_a:rows_a + p, :], w4_ref[0, 0 if back else 1]) * jnp.exp(-t * dl_ref[...])
            if blk == 0:
                taps = jnp.where(idx == 0, 0.0, taps)
            t_scr[blk * p:(blk + 1) * p, :] = taps
            total = total + jnp.sum(jnp.abs(taps), axis=0, keepdims=True)
        norm_scr[...] = jnp.broadcast_to(1.0 / total, norm_scr.shape)
        prev_scr[...] = block_dft(0)

    nxt = block_dft(di + 1)
    krow = lax.broadcasted_iota(jnp.int32, (2 * p, cb), 0)
    o_ref[0, 0] = (prev_scr[...] + jnp.where((krow & 1) == 0, nxt, -nxt)) * norm_scr[0:1, :]
    prev_scr[...] = nxt


def _hyena_filters(length, w1, b1, w2, b2, w3, b3, w4, freq):
    p = _conv_block(length)
    nb = length // p
    width = w4.shape[-1] // 4
    cb = min(FILTER_LANES, width)
    t = jnp.linspace(0.0, 1.0, length, dtype=F32)[:, None]
    bands = (HY_EMB_DIM - 1) // 2
    wpos = 2.0 * math.pi * jnp.arange(length, dtype=F32)[:, None] / length
    f = jnp.linspace(1e-4, bands - 1, bands, dtype=F32)[None, :]
    z = jnp.concatenate([t, jnp.cos(f * wpos), -jnp.sin(f * wpos)], axis=-1)
    z_back = jnp.concatenate([jnp.zeros((1, HY_EMB_DIM), F32), z[1:][::-1]], axis=0)
    hid = w2.shape[0]
    emb_pad = ((0, 0), (0, hid - HY_EMB_DIM))
    z_all = jnp.concatenate([jnp.pad(z_back, emb_pad), jnp.pad(z, emb_pad)], axis=1)
    both = lambda w: jnp.kron(jnp.eye(2, dtype=F32), w)
    twice = lambda v: jnp.tile(v.reshape(-1, hid), (1, 2))
    w1p = both(jnp.pad(w1, ((0, hid - HY_EMB_DIM), (0, 0))))
    w2, w3 = both(w2), both(w3)
    b1, b2, b3, freq = twice(b1), twice(b2), twice(b3), twice(freq)
    order = w4.shape[-1] // (2 * width)
    w4r = jnp.transpose(w4.reshape(w4.shape[0], order, 2, width), (1, 2, 0, 3))
    zero = jnp.zeros_like(w4r[:, 0])
    w4r = jnp.stack([jnp.concatenate([w4r[:, 1], zero], axis=1),
                     jnp.concatenate([zero, w4r[:, 0]], axis=1)], axis=1)
    hid = 2 * hid
    max_decay = math.log(HY_TARGET) / HY_FAST_DECAY
    min_decay = math.log(HY_TARGET) / HY_SLOW_DECAY
    deltas = jnp.abs(jnp.linspace(min_decay, max_decay, width, dtype=F32)).reshape(1, width)
    fwd = jnp.asarray(_dft_matrices(p)[0]).astype(BF16)
    full = lambda shape: pl.BlockSpec(shape, lambda o, c, di: (0,) * len(shape))
    return pl.pallas_call(
        functools.partial(_filter_kernel, length, p),
        grid=(order, width // cb, 2 * nb - 1),
        in_specs=[
            full(z_all.shape), full(w1p.shape), full((1, hid)), full(w2.shape), full((1, hid)),
            full(w3.shape), full((1, hid)), full(freq.shape),
            pl.BlockSpec((1, 2, hid, cb), lambda o, c, di: (o, 0, 0, c)),
            pl.BlockSpec((1, cb), lambda o, c, di: (0, c)),
            full(fwd.shape),
        ],
        out_specs=pl.BlockSpec((1, 1, 2 * p, cb), lambda o, c, di: (o, di, 0, c)),
        out_shape=jax.ShapeDtypeStruct((order, 2 * nb - 1, 2 * p, width), F32),
        scratch_shapes=[pltpu.VMEM((length, hid), F32), pltpu.VMEM((2 * length, cb), F32),
                        pltpu.VMEM((2 * p, cb), F32), pltpu.VMEM((SUBLANES, cb), F32)],
        compiler_params=_cparams(("arbitrary", "arbitrary", "arbitrary")),
        name="hyena_filters",
    )(z_all, w1p, b1, w2, b2, w3, b3, freq, w4r, deltas, fwd)


def _short_conv_block(pad_ref, j, p, w_ref, b_ref):
    base = SUBLANES + j * p
    return (pad_ref[base - 1: base - 1 + p, :] * w_ref[0:1, :] + pad_ref[base: base + p, :] * w_ref[1:2, :]
            + pad_ref[base + 1: base + 1 + p, :] * w_ref[2:3, :] + b_ref[...])


def _zero_halo(pad_ref, length):
    zeros = jnp.zeros((SUBLANES, pad_ref.shape[1]), F32)
    pad_ref[0:SUBLANES, :] = zeros
    pad_ref[SUBLANES + length: 2 * SUBLANES + length, :] = zeros


def _fill_rows(pad_ref, src_ref, start, stop):
    pad_ref[SUBLANES + start: SUBLANES + stop, :] = src_ref[start:stop, :].astype(F32)


def _conv_kernel(length, p, conv_u, u_ref, m_ref, h_ref, f_ref, g_ref, cwu_ref, cbu_ref, cwm_ref, cbm_ref,
                 skip_ref, o_ref, pad, uconv, uhat, yhat):
    nb = length // p
    nbat = u_ref.shape[0]
    lanes = lambda bi: slice(bi * LANES, (bi + 1) * LANES)

    for bi in range(nbat):
        _zero_halo(pad.at[bi], length)
        if conv_u:
            _fill_rows(pad.at[bi], u_ref.at[bi], 0, p)
    fwd = f_ref[...]
    for j in range(nb):
        blocks = []
        for bi in range(nbat):
            if conv_u:
                if j + 1 < nb:
                    _fill_rows(pad.at[bi], u_ref.at[bi], (j + 1) * p, (j + 2) * p)
                uj = _short_conv_block(pad.at[bi], j, p, cwu_ref, cbu_ref)
                uconv[bi, j * p:(j + 1) * p, :] = uj
            else:
                uj = u_ref[bi, j * p:(j + 1) * p, :]
            blocks.append(uj.astype(BF16))
        uhat[j] = _dot(fwd, jnp.concatenate(blocks, axis=1))
    for bi in range(nbat):
        _fill_rows(pad.at[bi], m_ref.at[bi], 0, p)

    inv = g_ref[...]
    first = lax.broadcasted_iota(jnp.int32, (SUBLANES, nbat * LANES), 0) == 0
    for i in range(nb):
        for c in range(p // ACC_ROWS):
            ra = slice(c * ACC_ROWS, (c + 1) * ACC_ROWS)
            rb = slice(p + c * ACC_ROWS, p + (c + 1) * ACC_ROWS)
            acc = [[jnp.zeros((ACC_ROWS, LANES), F32) for _ in range(2)] for _ in range(nbat)]
            for j in range(nb):
                h_d = h_ref.at[0, i - j + nb - 1]
                ha, hb = h_d[ra, :], h_d[rb, :]
                for bi in range(nbat):
                    ua, ub = uhat[j, ra, lanes(bi)], uhat[j, rb, lanes(bi)]
                    acc[bi][0] = acc[bi][0] + (ha * ua - hb * ub)
                    acc[bi][1] = acc[bi][1] + (ha * ub + hb * ua)
            for bi in range(nbat):
                yhat[ra, lanes(bi)] = acc[bi][0]
                yhat[rb, lanes(bi)] = acc[bi][1]
        dc = jnp.zeros((SUBLANES, nbat * LANES), F32)
        ny = jnp.zeros((SUBLANES, nbat * LANES), F32)
        for j in range(nb):
            h_d = h_ref.at[0, i - j + nb - 1]
            dc = dc + jnp.concatenate([h_d[0:SUBLANES, :]] * nbat, axis=1) * uhat[j, 0:SUBLANES, :]
            ny = ny + jnp.concatenate([h_d[p:p + SUBLANES, :]] * nbat, axis=1) * uhat[j, p:p + SUBLANES, :]
        yhat[0:SUBLANES, :] = jnp.where(first, dc, yhat[0:SUBLANES, :])
        yhat[p:p + SUBLANES, :] = jnp.where(first, ny, yhat[p:p + SUBLANES, :])

        y = _dot(inv, yhat[...].astype(BF16))
        for bi in range(nbat):
            if i + 1 < nb:
                _fill_rows(pad.at[bi], m_ref.at[bi], (i + 1) * p, (i + 2) * p)
            mi = _short_conv_block(pad.at[bi], i, p, cwm_ref, cbm_ref)
            src = uconv if conv_u else u_ref
            yi = y[:, lanes(bi)] + src[bi, i * p:(i + 1) * p, :].astype(F32) * skip_ref[...]
            o_ref[bi, i * p:(i + 1) * p, :] = (mi * yi).astype(o_ref.dtype)


def _long_conv_gate(u_src, u_col, conv_u, proj, m_col, spectra, order, conv_w, conv_b, skip, out_dtype):
    b, length, _ = proj.shape
    width = skip.shape[-1]
    p = _conv_block(length)
    nb = length // p
    cb = LANES
    ncb = width // cb
    nbat = CONV_BATCH if nb > 1 else b
    assert b % nbat == 0 and p % ACC_ROWS == 0
    fwd_np, inv_np = _dft_matrices(p)
    fwd = jnp.asarray(fwd_np).astype(BF16)
    inv = jnp.asarray(inv_np).astype(BF16)
    col = lambda base: (lambda c, bi: (bi, 0, base * ncb + c))
    par = lambda base: (lambda c, bi: (0, base * ncb + c))
    once = pl.Buffered(1)
    return pl.pallas_call(
        functools.partial(_conv_kernel, length, p, conv_u),
        grid=(ncb, b // nbat),
        in_specs=[
            pl.BlockSpec((nbat, length, cb), col(u_col)),
            pl.BlockSpec((nbat, length, cb), col(m_col)),
            pl.BlockSpec((1, 2 * nb - 1, 2 * p, cb), lambda c, bi: (order, 0, 0, c), pipeline_mode=once),
            pl.BlockSpec(fwd.shape, lambda c, bi: (0, 0), pipeline_mode=once),
            pl.BlockSpec(inv.shape, lambda c, bi: (0, 0), pipeline_mode=once),
            pl.BlockSpec((3, cb), par(0)),
            pl.BlockSpec((1, cb), par(0)),
            pl.BlockSpec((3, cb), par(m_col)),
            pl.BlockSpec((1, cb), par(m_col)),
            pl.BlockSpec((1, cb), lambda c, bi: (0, c)),
        ],
        out_specs=pl.BlockSpec((nbat, length, cb), lambda c, bi: (bi, 0, c)),
        out_shape=jax.ShapeDtypeStruct((b, length, width), out_dtype),
        scratch_shapes=[
            pltpu.VMEM((nbat, length + 2 * SUBLANES, cb), F32),
            pltpu.VMEM((nbat, length, cb) if conv_u else (nbat, SUBLANES, cb), F32),
            pltpu.VMEM((nb, 2 * p, nbat * cb), F32), pltpu.VMEM((2 * p, nbat * cb), F32),
        ],
        compiler_params=_cparams(("arbitrary", "arbitrary")),
        name="long_conv_gate",
    )(u_src, proj, spectra, fwd, inv, conv_w, conv_b.reshape(1, -1), conv_w, conv_b.reshape(1, -1),
      skip[order].reshape(1, width))


def _hyena_core(proj, spectra, conv_w, conv_b, skip):
    z = _long_conv_gate(proj, 0, True, proj, 1, spectra, 0, conv_w, conv_b, skip, F32)
    return _long_conv_gate(z, 0, False, proj, 2, spectra, 1, conv_w, conv_b, skip, BF16)


def kernel(x, c, ctx, c_ctx, w_ada, b_ada, w_in, w_out, ln_g, ln_b, na_rpb, hy_conv_w, hy_conv_b, hy_f_w1, hy_f_b1,
           hy_f_w2, hy_f_b2, hy_f_w3, hy_f_b3, hy_f_w4, hy_f_freq, hy_skip):
    depth = w_in.shape[0]
    b, s, d = x.shape
    n_ctx = ctx.shape[1]
    n_mixers = 2
    alpha = (2 * depth) ** 0.25

    cc = jnp.concatenate([c, c_ctx[None, :], jnp.zeros((2 * SUBLANES - b - 1, d), F32)], axis=0)
    mod = _ada_vectors(cc, w_ada, b_ada)
    w_in_bf = w_in.astype(BF16)
    w_out_bf = w_out.astype(BF16)
    mask_table = jnp.asarray(_na_mask_table(s // GRID_W))

    for i in range(depth):
        last = i == depth - 1
        j = i // n_mixers
        shift, scale, gate = [mod[i, :b, None, k * d:(k + 1) * d] for k in range(3)]
        shift_c, scale_c, gate_c = [jnp.broadcast_to(mod[i, b, k * d:(k + 1) * d], (b, 1, d)) for k in range(3)]
        proj_dtype = F32 if i % n_mixers == 0 else BF16
        proj = _in_projection(x, scale, shift, w_in_bf[i], proj_dtype)
        proj_c = None
        if i % n_mixers == 0 or not last:
            proj_c = _in_projection(ctx, scale_c, shift_c, w_in_bf[i], proj_dtype)
        if i % n_mixers == 0:
            bias_table = _na_bias_table(na_rpb[j])
            a = _neighbourhood_attention(proj, proj_c, bias_table, mask_table)
            a_c = None if last else _context_attention(proj_c)
        else:
            filt = (hy_f_w1[j], hy_f_b1[j], hy_f_w2[j], hy_f_b2[j], hy_f_w3[j], hy_f_b3[j], hy_f_w4[j], hy_f_freq[j])
            a = _hyena_core(proj, _hyena_filters(s, *filt), hy_conv_w[j], hy_conv_b[j], hy_skip[j])
            a_c = None if last else _hyena_core(proj_c, _hyena_filters(n_ctx, *filt), hy_conv_w[j], hy_conv_b[j],
                                                hy_skip[j])
        x = _out_projection(a, proj, x, gate, w_out_bf[i], ln_g[i], ln_b[i], alpha)
        if not last:
            ctx = _out_projection(a_c, proj_c, ctx, gate_c, w_out_bf[i], ln_g[i], ln_b[i], alpha)
    return x
```

```python
import functools
import math

import numpy as np
import jax
import jax.numpy as jnp
from jax import lax
from jax.experimental import pallas as pl
from jax.experimental.pallas import tpu as pltpu

F32 = jnp.float32
BF16 = jnp.bfloat16

HEAD_DIM = 64
GRID_W = 64
KH = 8
KW = 16
Q_BLOCK_W = 16
K_BLOCK_W = Q_BLOCK_W + KW
N_COL_BLOCKS = GRID_W // Q_BLOCK_W
RPB_ROWS = 2 * KH - 1
RPB_COLS = 2 * KW - 1
HY_EMB_DIM = 33
HY_FAST_DECAY = 0.3
HY_SLOW_DECAY = 1.5
HY_TARGET = 1e-2
LN_EPS = 1e-5
NEG_INF = -1e30
LOG2E = 1.4426950408889634

LANES = 128
SUBLANES = 8
HEADS_PER_TILE = LANES // HEAD_DIM
VMEM_LIMIT = 56 * 1024 * 1024

Q_ROWS = 8
NA_BLOCKS_PER_STEP = 4
OUT_PROJ_ROWS = 1024
K_ROWS = 16
ROWS_PER_LANE_TILE = LANES // K_BLOCK_W
N_DSTART = 28
DSTART_SHIFT = 8
CONV_BLOCK = 1024
CONV_BATCH = 2
ACC_ROWS = 32
FILTER_LANES = 256
IN_PROJ_ROWS = 512


def _cparams(sem):
    return pltpu.CompilerParams(dimension_semantics=sem, vmem_limit_bytes=VMEM_LIMIT)


def _split_bf16(a):
    hi = a.astype(BF16)
    lo = (a - hi.astype(F32)).astype(BF16)
    return hi, lo


def _dot(a, b):
    return jnp.dot(a, b, preferred_element_type=F32)


def _dot_nt(a, b):
    return lax.dot_general(a, b, (((1,), (1,)), ((), ())), preferred_element_type=F32)


def _dot3(a, b):
    ah, al = _split_bf16(a)
    bh, bl = _split_bf16(b)
    return _dot(ah, bh) + _dot(al, bh) + _dot(ah, bl)


def _silu(x):
    return x * jax.nn.sigmoid(x)


def _ada_kernel(cc_ref, w_ref, b_ref, o_ref):
    o_ref[0] = _dot3(_silu(cc_ref[...]), w_ref[0]) + b_ref[0]


def _ada_vectors(cc, w_ada, b_ada):
    depth, d, n = w_ada.shape
    tn = min(n, 1024)
    return pl.pallas_call(
        _ada_kernel,
        grid=(depth, n // tn),
        in_specs=[
            pl.BlockSpec(cc.shape, lambda i, j: (0, 0)),
            pl.BlockSpec((1, d, tn), lambda i, j: (i, 0, j)),
            pl.BlockSpec((1, 1, tn), lambda i, j: (i, 0, j)),
        ],
        out_specs=pl.BlockSpec((1, cc.shape[0], tn), lambda i, j: (i, 0, j)),
        out_shape=jax.ShapeDtypeStruct((depth, cc.shape[0], n), F32),
        compiler_params=_cparams(("arbitrary", "arbitrary")),
        name="ada_vectors",
    )(cc, w_ada, b_ada.reshape(depth, 1, n))


def _mod_spec(d, layer, mod_row, part):
    return pl.BlockSpec((1, 1, 1, d), lambda bi, i: (layer, bi if mod_row is None else mod_row, 0, part))


def _inproj_kernel(x_ref, sh_ref, sc_ref, w_ref, o_ref):
    h = (x_ref[0] * (1.0 + sc_ref[0, 0]) + sh_ref[0, 0]).astype(BF16)
    o_ref[0] = _dot(h, w_ref[0]).astype(o_ref.dtype)


def _in_projection(x, mod, layer, mod_row, w_bf16, out_dtype):
    b, s, d = x.shape
    n = w_bf16.shape[2]
    tm = min(s, IN_PROJ_ROWS)
    return pl.pallas_call(
        _inproj_kernel,
        grid=(b, s // tm),
        in_specs=[
            pl.BlockSpec((1, tm, d), lambda bi, i: (bi, i, 0)),
            _mod_spec(d, layer, mod_row, 0),
            _mod_spec(d, layer, mod_row, 1),
            pl.BlockSpec((1, d, n), lambda bi, i: (layer, 0, 0), pipeline_mode=pl.Buffered(1)),
        ],
        out_specs=pl.BlockSpec((1, tm, n), lambda bi, i: (bi, i, 0)),
        out_shape=jax.ShapeDtypeStruct((b, s, n), out_dtype),
        compiler_params=_cparams(("arbitrary", "arbitrary")),
        name="in_projection",
    )(x, mod, mod, w_bf16)


def _outproj_kernel(alpha, a_ref, z_ref, x_ref, gate_ref, w_ref, g_ref, b_ref, o_ref):
    a = a_ref[0].astype(F32) * _silu(z_ref[0].astype(F32))
    y = _dot(a.astype(BF16), w_ref[0])
    r = alpha * x_ref[0] + gate_ref[0, 0] * y
    mu = jnp.mean(r, axis=-1, keepdims=True)
    dlt = r - mu
    var = jnp.mean(dlt * dlt, axis=-1, keepdims=True)
    o_ref[0] = dlt * lax.rsqrt(var + LN_EPS) * g_ref[0] + b_ref[0]


def _out_projection(a, proj, x, mod, layer, mod_row, w_bf16, ln_g, ln_b, alpha):
    b, s, d = x.shape
    tm = min(s, OUT_PROJ_ROWS)
    row = lambda bi, i: (bi, i, 0)
    per_layer = lambda bi, i: (layer, 0, 0)
    return pl.pallas_call(
        functools.partial(_outproj_kernel, alpha),
        grid=(b, s // tm),
        in_specs=[
            pl.BlockSpec((1, tm, d), row),
            pl.BlockSpec((1, tm, d), lambda bi, i: (bi, i, 3)),
            pl.BlockSpec((1, tm, d), row),
            _mod_spec(d, layer, mod_row, 2),
            pl.BlockSpec((1, d, d), per_layer),
            pl.BlockSpec((1, 1, d), per_layer),
            pl.BlockSpec((1, 1, d), per_layer),
        ],
        out_specs=pl.BlockSpec((1, tm, d), row),
        out_shape=jax.ShapeDtypeStruct((b, s, d), F32),
        compiler_params=_cparams(("arbitrary", "arbitrary")),
        name="out_projection",
    )(a, proj, x, mod, w_bf16, ln_g, ln_b)


def _col_layout():
    q_cols = np.arange(GRID_W).reshape(N_COL_BLOCKS, Q_BLOCK_W)
    q_start = np.clip(q_cols - KW // 2, 0, GRID_W - KW)
    blk_start = np.clip(np.arange(N_COL_BLOCKS) * Q_BLOCK_W - KW // 2, 0, GRID_W - K_BLOCK_W)
    k_cols = blk_start[:, None] + np.arange(K_BLOCK_W)
    kc = k_cols[:, None, :]
    in_win = (kc >= q_start[:, :, None]) & (kc < q_start[:, :, None] + KW)
    dcol = np.clip(kc - q_cols[:, :, None] + KW - 1, 0, RPB_COLS - 1)
    return blk_start, in_win, dcol


def _window_base(rb, rows):
    return np.clip(rb * Q_ROWS - KH // 2, 0, rows - K_ROWS)


def _na_mask_table(rows):
    _, in_win, _ = _col_layout()
    n_rb = rows // Q_ROWS
    out = np.zeros((3, N_COL_BLOCKS, Q_ROWS, K_ROWS // ROWS_PER_LANE_TILE, Q_BLOCK_W, LANES), np.float32)
    for cls, rb in enumerate((0, 1, n_rb - 1)):
        kr0 = _window_base(rb, rows)
        for qr in range(Q_ROWS):
            r = rb * Q_ROWS + qr
            r0 = np.clip(r - KH // 2, 0, rows - KH)
            for kr in range(K_ROWS):
                row_ok = r0 <= kr0 + kr < r0 + KH
                g, j = divmod(kr, ROWS_PER_LANE_TILE)
                ok = in_win & row_ok
                out[cls, :, qr, g, :, j * K_BLOCK_W:(j + 1) * K_BLOCK_W] = np.where(ok, 0.0, NEG_INF)
    return out


def _rpb_expand_kernel(r_ref, e_ref, o_ref):
    acc = None
    for j in range(ROWS_PER_LANE_TILE):
        r = r_ref[j]
        hi = r.astype(BF16)
        r1 = r - hi.astype(F32)
        mid = r1.astype(BF16)
        lo = (r1 - mid.astype(F32)).astype(BF16)
        e = e_ref[0, j]
        part = _dot(hi, e) + _dot(mid, e) + _dot(lo, e)
        acc = part if acc is None else acc + part
    o_ref[0] = acc * LOG2E


def _na_bias_table(rpb):
    h = rpb.shape[0]
    _, _, dcol = _col_layout()
    kpad = 32
    nj = ROWS_PER_LANE_TILE
    sel = dcol[:, None, :, :] == np.arange(kpad)[None, :, None, None]
    onehot = np.zeros((N_COL_BLOCKS, nj, kpad, Q_BLOCK_W, nj, K_BLOCK_W), np.float32)
    for j in range(nj):
        onehot[:, j, :, :, j, :] = sel
    onehot = onehot.reshape(N_COL_BLOCKS, nj, kpad, Q_BLOCK_W * LANES)
    hi_pad = N_DSTART + nj - 1 - DSTART_SHIFT - RPB_ROWS
    rp = jnp.pad(rpb, ((0, 0), (DSTART_SHIFT, hi_pad), (0, kpad - RPB_COLS)))
    shifted = jnp.stack([rp[:, j:j + N_DSTART] for j in range(nj)], axis=0).reshape(nj, h * N_DSTART, kpad)
    t = pl.pallas_call(
        _rpb_expand_kernel,
        grid=(N_COL_BLOCKS,),
        in_specs=[
            pl.BlockSpec(shifted.shape, lambda n: (0, 0, 0)),
            pl.BlockSpec((1, nj, kpad, Q_BLOCK_W * LANES), lambda n: (n, 0, 0, 0)),
        ],
        out_specs=pl.BlockSpec((1, h * N_DSTART, Q_BLOCK_W * LANES), lambda n: (n, 0, 0)),
        out_shape=jax.ShapeDtypeStruct((N_COL_BLOCKS, h * N_DSTART, Q_BLOCK_W * LANES), F32),
        compiler_params=_cparams(("arbitrary",)),
        name="rpb_expand",
    )(shifted, jnp.asarray(onehot, BF16))
    return t.reshape(N_COL_BLOCKS, h // HEADS_PER_TILE, HEADS_PER_TILE, N_DSTART, Q_BLOCK_W, LANES)


def _na_kernel(rows, blk_start, q_ref, k_ref, v_ref, kc_ref, vc_ref, t_ref, m_ref, o_ref):
    n_rb = rows // Q_ROWS
    lane = lax.broadcasted_iota(jnp.int32, (1, LANES), 1)
    kctx = kc_ref[0].astype(BF16)
    vctx = vc_ref[0].astype(BF16)
    n_groups = K_ROWS // ROWS_PER_LANE_TILE
    nq = Q_ROWS * Q_BLOCK_W
    rows_n = HEADS_PER_TILE * nq

    def prepare(sub):
        rb = pl.program_id(2) * NA_BLOCKS_PER_STEP + sub
        kr0 = jnp.clip(rb * Q_ROWS - KH // 2, 0, rows - K_ROWS)
        cls = jnp.where(rb == 0, 0, jnp.where(rb == n_rb - 1, 2, 1))
        off = kr0 - rb * Q_ROWS + KH - 1 + DSTART_SHIFT
        q0 = sub * Q_ROWS * GRID_W
        q2s = []
        for n in range(N_COL_BLOCKS):
            qn = jnp.concatenate(
                [q_ref[0, q0 + qr * GRID_W + n * Q_BLOCK_W: q0 + qr * GRID_W + (n + 1) * Q_BLOCK_W, :]
                 for qr in range(Q_ROWS)], axis=0) * (HEAD_DIM ** -0.5 * LOG2E)
            q2s.append(jnp.concatenate(
                [jnp.where((lane >= h * HEAD_DIM) & (lane < (h + 1) * HEAD_DIM), qn, 0.0)
                 for h in range(HEADS_PER_TILE)], axis=0).astype(BF16))
        s_ctx_all = _dot_nt(jnp.concatenate(q2s, axis=0), kctx)
        return kr0, cls, off, q0, q2s, s_ctx_all

    def scores(prep, n):
        kr0, cls, off, _, q2s, s_ctx_all = prep
        c0 = int(blk_start[n])
        window = lambda ref: jnp.concatenate(
            [ref[0, pl.ds(pl.multiple_of((kr0 + kr) * GRID_W + c0, SUBLANES), K_BLOCK_W), :] for kr in range(K_ROWS)],
            axis=0).astype(BF16)
        bias = jnp.concatenate([
            jnp.concatenate([t_ref[n, 0, h, off + ROWS_PER_LANE_TILE * g - qr] + m_ref[cls, n, qr, g]
                             for g in range(n_groups)], axis=1)
            for h in range(HEADS_PER_TILE) for qr in range(Q_ROWS)], axis=0)
        return _dot_nt(q2s[n], window(k_ref)) + bias, s_ctx_all[n * rows_n:(n + 1) * rows_n], window(v_ref)

    def finish(q0, n, s_lat, s_ctx, vn):
        m = jnp.maximum(jnp.max(s_lat, axis=-1, keepdims=True), jnp.max(s_ctx, axis=-1, keepdims=True))
        e_lat = jnp.exp2(s_lat - m)
        e_ctx = jnp.exp2(s_ctx - m)
        den = jnp.sum(e_lat, axis=-1, keepdims=True) + jnp.sum(e_ctx, axis=-1, keepdims=True)
        o2 = (_dot(e_lat.astype(BF16), vn) + _dot(e_ctx.astype(BF16), vctx)) / den
        o_n = o2[0:nq]
        for h in range(1, HEADS_PER_TILE):
            o_n = jnp.where((lane >= h * HEAD_DIM) & (lane < (h + 1) * HEAD_DIM), o2[h * nq:(h + 1) * nq], o_n)
        for qr in range(Q_ROWS):
            o_ref[0, q0 + qr * GRID_W + n * Q_BLOCK_W: q0 + qr * GRID_W + (n + 1) * Q_BLOCK_W, :] = (
                o_n[qr * Q_BLOCK_W:(qr + 1) * Q_BLOCK_W].astype(o_ref.dtype))

    order = [(sub, n) for sub in range(NA_BLOCKS_PER_STEP) for n in range(N_COL_BLOCKS)]
    preps = {0: prepare(0)}
    cur = scores(preps[0], 0)
    for idx, (sub, n) in enumerate(order):
        nxt = None
        if idx + 1 < len(order):
            nsub, nn = order[idx + 1]
            if nsub not in preps:
                preps[nsub] = prepare(nsub)
            nxt = scores(preps[nsub], nn)
        finish(preps[sub][3], n, *cur)
        cur = nxt


def _neighbourhood_attention(proj, proj_c, bias_table, mask_table):
    b, s, d4 = proj.shape
    d = d4 // 4
    c = proj_c.shape[1]
    rows = s // GRID_W
    n_tiles = d // LANES
    blk_start, _, _ = _col_layout()
    tq = NA_BLOCKS_PER_STEP * Q_ROWS * GRID_W
    return pl.pallas_call(
        functools.partial(_na_kernel, rows, blk_start),
        grid=(b, n_tiles, s // tq),
        in_specs=[
            pl.BlockSpec((1, tq, LANES), lambda bi, hp, rb: (bi, rb, hp)),
            pl.BlockSpec((1, s, LANES), lambda bi, hp, rb: (bi, 0, n_tiles + hp)),
            pl.BlockSpec((1, s, LANES), lambda bi, hp, rb: (bi, 0, 2 * n_tiles + hp)),
            pl.BlockSpec((1, c, LANES), lambda bi, hp, rb: (bi, 0, n_tiles + hp)),
            pl.BlockSpec((1, c, LANES), lambda bi, hp, rb: (bi, 0, 2 * n_tiles + hp)),
            pl.BlockSpec(bias_table.shape[:1] + (1,) + bias_table.shape[2:], lambda bi, hp, rb: (0, hp, 0, 0, 0, 0)),
            pl.BlockSpec(mask_table.shape, lambda bi, hp, rb: (0, 0, 0, 0, 0, 0)),
        ],
        out_specs=pl.BlockSpec((1, tq, LANES), lambda bi, hp, rb: (bi, rb, hp)),
        out_shape=jax.ShapeDtypeStruct((b, s, d), BF16),
        compiler_params=_cparams(("arbitrary", "arbitrary", "arbitrary")),
        name="neighbourhood_attention",
    )(proj, proj, proj, proj_c, proj_c, bias_table, mask_table)


def _ctx_attn_kernel(q_ref, k_ref, v_ref, o_ref):
    lane = lax.broadcasted_iota(jnp.int32, (1, LANES), 1)
    for t in range(q_ref.shape[-1] // LANES):
        cols = slice(t * LANES, (t + 1) * LANES)
        q = q_ref[0, :, cols] * (HEAD_DIM ** -0.5 * LOG2E)
        k = k_ref[0, :, cols].astype(BF16)
        v = v_ref[0, :, cols].astype(BF16)
        out = None
        for h in range(HEADS_PER_TILE):
            in_head = (lane >= h * HEAD_DIM) & (lane < (h + 1) * HEAD_DIM)
            s = _dot_nt(jnp.where(in_head, q, 0.0).astype(BF16), k)
            e = jnp.exp2(s - jnp.max(s, axis=-1, keepdims=True))
            o_h = _dot(e.astype(BF16), v) / jnp.sum(e, axis=-1, keepdims=True)
            out = o_h if out is None else jnp.where(in_head, o_h, out)
        o_ref[0, :, cols] = out.astype(o_ref.dtype)


def _context_attention(proj_c):
    b, c, d4 = proj_c.shape
    d = d4 // 4
    return pl.pallas_call(
        _ctx_attn_kernel,
        grid=(b,),
        in_specs=[
            pl.BlockSpec((1, c, d), lambda bi: (bi, 0, 0)),
            pl.BlockSpec((1, c, d), lambda bi: (bi, 0, 1)),
            pl.BlockSpec((1, c, d), lambda bi: (bi, 0, 2)),
        ],
        out_specs=pl.BlockSpec((1, c, d), lambda bi: (bi, 0, 0)),
        out_shape=jax.ShapeDtypeStruct((b, c, d), BF16),
        compiler_params=_cparams(("arbitrary",)),
        name="context_attention",
    )(proj_c, proj_c, proj_c)


def _conv_block(length):
    return min(CONV_BLOCK, length)


def _dft_matrices(p):
    k = np.arange(p, dtype=np.float64)[:, None]
    n = np.arange(p, dtype=np.float64)[None, :]
    ang = 2.0 * np.pi * k * n / (2 * p)
    f_im = -np.sin(ang)
    f_im[0] = np.cos(np.pi * n[0])
    fwd = np.concatenate([np.cos(ang), f_im], axis=0)
    wk = np.where(k == 0, 1.0, 2.0) / (2 * p)
    ang_i = 2.0 * np.pi * k * (n + p) / (2 * p)
    g_im = -np.sin(ang_i) * wk
    g_im[0] = np.cos(np.pi * (n[0] + p)) / (2 * p)
    inv = np.concatenate([np.cos(ang_i) * wk, g_im], axis=0).T
    return fwd.astype(np.float32), inv.astype(np.float32)


def _filter_kernel(length, p, z_ref, w1_ref, b1_ref, w2_ref, b2_ref, w3_ref, b3_ref, fr_ref, w4_ref,
                   dl_ref, f_ref, o_ref, a_scr, t_scr, prev_scr, norm_scr):
    nb = length // p
    di = pl.program_id(2)
    cb = o_ref.shape[-1]

    @pl.when((pl.program_id(0) == 0) & (pl.program_id(1) == 0) & (di == 0))
    def _():
        a = jnp.sin(fr_ref[0:1, :] * (_dot3(z_ref[...], w1_ref[...]) + b1_ref[...]))
        a = jnp.sin(fr_ref[1:2, :] * (_dot3(a, w2_ref[...]) + b2_ref[...]))
        a_scr[...] = jnp.sin(fr_ref[2:3, :] * (_dot3(a, w3_ref[...]) + b3_ref[...]))

    def block_dft(blk):
        start = pl.multiple_of(blk * p, p)
        return _dot(f_ref[...], t_scr[pl.ds(start, p), :].astype(BF16))

    @pl.when(di == 0)
    def _():
        total = jnp.zeros((1, cb), F32)
        for blk in range(2 * nb):
            back = blk < nb
            idx = lax.broadcasted_iota(jnp.int32, (p, cb), 0) + (blk % nb) * p
            t = ((length - idx) if back else idx).astype(F32) * (1.0 / (length - 1))
            rows_a = (blk % nb) * p
            taps = _dot3(a_scr[rows_a:rows_a + p, :], w4_ref[0, 0 if back else 1]) * jnp.exp(-t * dl_ref[...])
            if blk == 0:
                taps = jnp.where(idx == 0, 0.0, taps)
            t_scr[blk * p:(blk + 1) * p, :] = taps
            total = total + jnp.sum(jnp.abs(taps), axis=0, keepdims=True)
        norm_scr[...] = jnp.broadcast_to(1.0 / total, norm_scr.shape)
        prev_scr[...] = block_dft(0)

    nxt = block_dft(di + 1)
    krow = lax.broadcasted_iota(jnp.int32, (2 * p, cb), 0)
    o_ref[0, 0] = (prev_scr[...] + jnp.where((krow & 1) == 0, nxt, -nxt)) * norm_scr[0:1, :]
    prev_scr[...] = nxt


def _hyena_filters(length, w1, b1, w2, b2, w3, b3, w4, freq):
    p = _conv_block(length)
    nb = length // p
    width = w4.shape[-1] // 4
    cb = min(FILTER_LANES, width)
    t = jnp.linspace(0.0, 1.0, length, dtype=F32)[:, None]
    bands = (HY_EMB_DIM - 1) // 2
    wpos = 2.0 * math.pi * jnp.arange(length, dtype=F32)[:, None] / length
    f = jnp.linspace(1e-4, bands - 1, bands, dtype=F32)[None, :]
    z = jnp.concatenate([t, jnp.cos(f * wpos), -jnp.sin(f * wpos)], axis=-1)
    z_back = jnp.concatenate([jnp.zeros((1, HY_EMB_DIM), F32), z[1:][::-1]], axis=0)
    hid = w2.shape[0]
    emb_pad = ((0, 0), (0, hid - HY_EMB_DIM))
    z_all = jnp.concatenate([jnp.pad(z_back, emb_pad), jnp.pad(z, emb_pad)], axis=1)
    both = lambda w: jnp.kron(jnp.eye(2, dtype=F32), w)
    twice = lambda v: jnp.tile(v.reshape(-1, hid), (1, 2))
    w1p = both(jnp.pad(w1, ((0, hid - HY_EMB_DIM), (0, 0))))
    w2, w3 = both(w2), both(w3)
    b1, b2, b3, freq = twice(b1), twice(b2), twice(b3), twice(freq)
    order = w4.shape[-1] // (2 * width)
    w4r = jnp.transpose(w4.reshape(w4.shape[0], order, 2, width), (1, 2, 0, 3))
    zero = jnp.zeros_like(w4r[:, 0])
    w4r = jnp.stack([jnp.concatenate([w4r[:, 1], zero], axis=1),
                     jnp.concatenate([zero, w4r[:, 0]], axis=1)], axis=1)
    hid = 2 * hid
    max_decay = math.log(HY_TARGET) / HY_FAST_DECAY
    min_decay = math.log(HY_TARGET) / HY_SLOW_DECAY
    deltas = jnp.abs(jnp.linspace(min_decay, max_decay, width, dtype=F32)).reshape(1, width)
    fwd = jnp.asarray(_dft_matrices(p)[0]).astype(BF16)
    full = lambda shape: pl.BlockSpec(shape, lambda o, c, di: (0,) * len(shape))
    return pl.pallas_call(
        functools.partial(_filter_kernel, length, p),
        grid=(order, width // cb, 2 * nb - 1),
        in_specs=[
            full(z_all.shape), full(w1p.shape), full((1, hid)), full(w2.shape), full((1, hid)),
            full(w3.shape), full((1, hid)), full(freq.shape),
            pl.BlockSpec((1, 2, hid, cb), lambda o, c, di: (o, 0, 0, c)),
            pl.BlockSpec((1, cb), lambda o, c, di: (0, c)),
            full(fwd.shape),
        ],
        out_specs=pl.BlockSpec((1, 1, 2 * p, cb), lambda o, c, di: (o, di, 0, c)),
        out_shape=jax.ShapeDtypeStruct((order, 2 * nb - 1, 2 * p, width), F32),
        scratch_shapes=[pltpu.VMEM((length, hid), F32), pltpu.VMEM((2 * length, cb), F32),
                        pltpu.VMEM((2 * p, cb), F32), pltpu.VMEM((SUBLANES, cb), F32)],
        compiler_params=_cparams(("arbitrary", "arbitrary", "arbitrary")),
        name="hyena_filters",
    )(z_all, w1p, b1, w2, b2, w3, b3, freq, w4r, deltas, fwd)


def _short_conv_block(pad_ref, j, p, w_ref, b_ref):
    base = SUBLANES + j * p
    return (pad_ref[base - 1: base - 1 + p, :] * w_ref[0:1, :] + pad_ref[base: base + p, :] * w_ref[1:2, :]
            + pad_ref[base + 1: base + 1 + p, :] * w_ref[2:3, :] + b_ref[...])


def _zero_halo(pad_ref, length):
    zeros = jnp.zeros((SUBLANES, pad_ref.shape[1]), F32)
    pad_ref[0:SUBLANES, :] = zeros
    pad_ref[SUBLANES + length: 2 * SUBLANES + length, :] = zeros


def _fill_rows(pad_ref, src_ref, start, stop):
    pad_ref[SUBLANES + start: SUBLANES + stop, :] = src_ref[start:stop, :].astype(F32)


def _conv_kernel(length, p, conv_u, u_ref, m_ref, h_ref, f_ref, g_ref, cwu_ref, cbu_ref, cwm_ref, cbm_ref,
                 skip_ref, o_ref, pad, uconv, uhat, yhat):
    nb = length // p
    nbat = u_ref.shape[0]
    lanes = lambda bi: slice(bi * LANES, (bi + 1) * LANES)

    for bi in range(nbat):
        _zero_halo(pad.at[bi], length)
        if conv_u:
            _fill_rows(pad.at[bi], u_ref.at[bi], 0, p)
    fwd = f_ref[...]
    for j in range(nb):
        blocks = []
        for bi in range(nbat):
            if conv_u:
                if j + 1 < nb:
                    _fill_rows(pad.at[bi], u_ref.at[bi], (j + 1) * p, (j + 2) * p)
                uj = _short_conv_block(pad.at[bi], j, p, cwu_ref, cbu_ref)
                uconv[bi, j * p:(j + 1) * p, :] = uj
            else:
                uj = u_ref[bi, j * p:(j + 1) * p, :]
            blocks.append(uj.astype(BF16))
        uhat[j] = _dot(fwd, jnp.concatenate(blocks, axis=1))
    for bi in range(nbat):
        _fill_rows(pad.at[bi], m_ref.at[bi], 0, p)

    inv = g_ref[...]
    first = lax.broadcasted_iota(jnp.int32, (SUBLANES, nbat * LANES), 0) == 0
    for i in range(nb):
        for c in range(p // ACC_ROWS):
            ra = slice(c * ACC_ROWS, (c + 1) * ACC_ROWS)
            rb = slice(p + c * ACC_ROWS, p + (c + 1) * ACC_ROWS)
            acc = [[jnp.zeros((ACC_ROWS, LANES), F32) for _ in range(2)] for _ in range(nbat)]
            for j in range(nb):
                h_d = h_ref.at[0, i - j + nb - 1]
                ha, hb = h_d[ra, :], h_d[rb, :]
                for bi in range(nbat):
                    ua, ub = uhat[j, ra, lanes(bi)], uhat[j, rb, lanes(bi)]
                    acc[bi][0] = acc[bi][0] + (ha * ua - hb * ub)
                    acc[bi][1] = acc[bi][1] + (ha * ub + hb * ua)
            for bi in range(nbat):
                yhat[ra, lanes(bi)] = acc[bi][0]
                yhat[rb, lanes(bi)] = acc[bi][1]
        dc = jnp.zeros((SUBLANES, nbat * LANES), F32)
        ny = jnp.zeros((SUBLANES, nbat * LANES), F32)
        for j in range(nb):
            h_d = h_ref.at[0, i - j + nb - 1]
            dc = dc + jnp.concatenate([h_d[0:SUBLANES, :]] * nbat, axis=1) * uhat[j, 0:SUBLANES, :]
            ny = ny + jnp.concatenate([h_d[p:p + SUBLANES, :]] * nbat, axis=1) * uhat[j, p:p + SUBLANES, :]
        yhat[0:SUBLANES, :] = jnp.where(first, dc, yhat[0:SUBLANES, :])
        yhat[p:p + SUBLANES, :] = jnp.where(first, ny, yhat[p:p + SUBLANES, :])

        y = _dot(inv, yhat[...].astype(BF16))
        for bi in range(nbat):
            if i + 1 < nb:
                _fill_rows(pad.at[bi], m_ref.at[bi], (i + 1) * p, (i + 2) * p)
            mi = _short_conv_block(pad.at[bi], i, p, cwm_ref, cbm_ref)
            src = uconv if conv_u else u_ref
            yi = y[:, lanes(bi)] + src[bi, i * p:(i + 1) * p, :].astype(F32) * skip_ref[...]
            o_ref[bi, i * p:(i + 1) * p, :] = (mi * yi).astype(o_ref.dtype)


def _long_conv_gate(u_src, u_col, conv_u, proj, m_col, spectra, order, conv_w, conv_b, skip, out_dtype):
    b, length, _ = proj.shape
    width = skip.shape[-1]
    p = _conv_block(length)
    nb = length // p
    cb = LANES
    ncb = width // cb
    nbat = CONV_BATCH if nb > 1 else b
    assert b % nbat == 0 and p % ACC_ROWS == 0
    fwd_np, inv_np = _dft_matrices(p)
    fwd = jnp.asarray(fwd_np).astype(BF16)
    inv = jnp.asarray(inv_np).astype(BF16)
    col = lambda base: (lambda c, bi: (bi, 0, base * ncb + c))
    par = lambda base: (lambda c, bi: (0, base * ncb + c))
    once = pl.Buffered(1)
    return pl.pallas_call(
        functools.partial(_conv_kernel, length, p, conv_u),
        grid=(ncb, b // nbat),
        in_specs=[
            pl.BlockSpec((nbat, length, cb), col(u_col)),
            pl.BlockSpec((nbat, length, cb), col(m_col)),
            pl.BlockSpec((1, 2 * nb - 1, 2 * p, cb), lambda c, bi: (order, 0, 0, c), pipeline_mode=once),
            pl.BlockSpec(fwd.shape, lambda c, bi: (0, 0), pipeline_mode=once),
            pl.BlockSpec(inv.shape, lambda c, bi: (0, 0), pipeline_mode=once),
            pl.BlockSpec((3, cb), par(0)),
            pl.BlockSpec((1, cb), par(0)),
            pl.BlockSpec((3, cb), par(m_col)),
            pl.BlockSpec((1, cb), par(m_col)),
            pl.BlockSpec((1, cb), lambda c, bi: (0, c)),
        ],
        out_specs=pl.BlockSpec((nbat, length, cb), lambda c, bi: (bi, 0, c)),
        out_shape=jax.ShapeDtypeStruct((b, length, width), out_dtype),
        scratch_shapes=[
            pltpu.VMEM((nbat, length + 2 * SUBLANES, cb), F32),
            pltpu.VMEM((nbat, length, cb) if conv_u else (nbat, SUBLANES, cb), F32),
            pltpu.VMEM((nb, 2 * p, nbat * cb), F32), pltpu.VMEM((2 * p, nbat * cb), F32),
        ],
        compiler_params=_cparams(("arbitrary", "arbitrary")),
        name="long_conv_gate",
    )(u_src, proj, spectra, fwd, inv, conv_w, conv_b.reshape(1, -1), conv_w, conv_b.reshape(1, -1),
      skip[order].reshape(1, width))


def _hyena_core(proj, spectra, conv_w, conv_b, skip):
    z = _long_conv_gate(proj, 0, True, proj, 1, spectra, 0, conv_w, conv_b, skip, F32)
    return _long_conv_gate(z, 0, False, proj, 2, spectra, 1, conv_w, conv_b, skip, BF16)


def kernel(x, c, ctx, c_ctx, w_ada, b_ada, w_in, w_out, ln_g, ln_b, na_rpb, hy_conv_w, hy_conv_b, hy_f_w1, hy_f_b1,
           hy_f_w2, hy_f_b2, hy_f_w3, hy_f_b3, hy_f_w4, hy_f_freq, hy_skip):
    depth = w_in.shape[0]
    b, s, d = x.shape
    n_ctx = ctx.shape[1]
    n_mixers = 2
    alpha = (2 * depth) ** 0.25

    cc = jnp.concatenate([c, c_ctx[None, :], jnp.zeros((2 * SUBLANES - b - 1, d), F32)], axis=0)
    mod = _ada_vectors(cc, w_ada, b_ada)
    mod = mod.reshape(depth, mod.shape[1], 1, 3 * d)
    w_in_bf = w_in.astype(BF16)
    w_out_bf = w_out.astype(BF16)
    ln_g = ln_g.reshape(depth, 1, d)
    ln_b = ln_b.reshape(depth, 1, d)
    mask_table = jnp.asarray(_na_mask_table(s // GRID_W))

    for i in range(depth):
        last = i == depth - 1
        j = i // n_mixers
        proj_dtype = F32 if i % n_mixers == 0 else BF16
        proj = _in_projection(x, mod, i, None, w_in_bf, proj_dtype)
        proj_c = None
        if i % n_mixers == 0 or not last:
            proj_c = _in_projection(ctx, mod, i, b, w_in_bf, proj_dtype)
        if i % n_mixers == 0:
            bias_table = _na_bias_table(na_rpb[j])
            a = _neighbourhood_attention(proj, proj_c, bias_table, mask_table)
            a_c = None if last else _context_attention(proj_c)
        else:
            filt = (hy_f_w1[j], hy_f_b1[j], hy_f_w2[j], hy_f_b2[j], hy_f_w3[j], hy_f_b3[j], hy_f_w4[j], hy_f_freq[j])
            a = _hyena_core(proj, _hyena_filters(s, *filt), hy_conv_w[j], hy_conv_b[j], hy_skip[j])
            a_c = None if last else _hyena_core(proj_c, _hyena_filters(n_ctx, *filt), hy_conv_w[j], hy_conv_b[j],
                                                hy_skip[j])
        x = _out_projection(a, proj, x, mod, i, None, w_out_bf, ln_g, ln_b, alpha)
        if not last:
            ctx = _out_projection(a_c, proj_c, ctx, mod, i, b, w_out_bf, ln_g, ln_b, alpha)
    return x
```

```python
import functools
import math

import numpy as np
import jax
import jax.numpy as jnp
from jax import lax
from jax.experimental import pallas as pl
from jax.experimental.pallas import tpu as pltpu

F32 = jnp.float32
BF16 = jnp.bfloat16

HEAD_DIM = 64
GRID_W = 64
KH = 8
KW = 16
Q_BLOCK_W = 16
K_BLOCK_W = Q_BLOCK_W + KW
N_COL_BLOCKS = GRID_W // Q_BLOCK_W
RPB_ROWS = 2 * KH - 1
RPB_COLS = 2 * KW - 1
HY_EMB_DIM = 33
HY_FAST_DECAY = 0.3
HY_SLOW_DECAY = 1.5
HY_TARGET = 1e-2
LN_EPS = 1e-5
NEG_INF = -1e30
LOG2E = 1.4426950408889634

LANES = 128
SUBLANES = 8
HEADS_PER_TILE = LANES // HEAD_DIM
VMEM_LIMIT = 56 * 1024 * 1024

Q_ROWS = 8
NA_BLOCKS_PER_STEP = 8
OUT_PROJ_ROWS = 1024
K_ROWS = 16
ROWS_PER_LANE_TILE = LANES // K_BLOCK_W
N_DSTART = 28
DSTART_SHIFT = 8
CONV_BLOCK = 1024
CONV_BATCH = 2
ACC_ROWS = 32
FILTER_LANES = 256
IN_PROJ_ROWS = 512


def _cparams(sem):
    return pltpu.CompilerParams(dimension_semantics=sem, vmem_limit_bytes=VMEM_LIMIT)


def _split_bf16(a):
    hi = a.astype(BF16)
    lo = (a - hi.astype(F32)).astype(BF16)
    return hi, lo


def _dot(a, b):
    return jnp.dot(a, b, preferred_element_type=F32)


def _dot_nt(a, b):
    return lax.dot_general(a, b, (((1,), (1,)), ((), ())), preferred_element_type=F32)


def _dot3(a, b):
    ah, al = _split_bf16(a)
    bh, bl = _split_bf16(b)
    return _dot(ah, bh) + _dot(al, bh) + _dot(ah, bl)


def _silu(x):
    return x * jax.nn.sigmoid(x)


def _ada_kernel(cc_ref, w_ref, b_ref, o_ref):
    o_ref[0] = _dot3(_silu(cc_ref[...]), w_ref[0]) + b_ref[0]


def _ada_vectors(cc, w_ada, b_ada):
    depth, d, n = w_ada.shape
    tn = min(n, 1024)
    return pl.pallas_call(
        _ada_kernel,
        grid=(depth, n // tn),
        in_specs=[
            pl.BlockSpec(cc.shape, lambda i, j: (0, 0)),
            pl.BlockSpec((1, d, tn), lambda i, j: (i, 0, j)),
            pl.BlockSpec((1, 1, tn), lambda i, j: (i, 0, j)),
        ],
        out_specs=pl.BlockSpec((1, cc.shape[0], tn), lambda i, j: (i, 0, j)),
        out_shape=jax.ShapeDtypeStruct((depth, cc.shape[0], n), F32),
        compiler_params=_cparams(("arbitrary", "arbitrary")),
        name="ada_vectors",
    )(cc, w_ada, b_ada.reshape(depth, 1, n))


def _mod_spec(d, layer, mod_row, part):
    return pl.BlockSpec((1, 1, 1, d), lambda bi, i: (layer, bi if mod_row is None else mod_row, 0, part))


def _first_step():
    return (pl.program_id(0) == 0) & (pl.program_id(1) == 0)


def _inproj_kernel(x_ref, sh_ref, sc_ref, w_ref, o_ref, w_bf):
    @pl.when(_first_step())
    def _():
        w_bf[...] = w_ref[0].astype(BF16)

    h = (x_ref[0] * (1.0 + sc_ref[0, 0]) + sh_ref[0, 0]).astype(BF16)
    o_ref[0] = _dot(h, w_bf[...]).astype(o_ref.dtype)


def _in_projection(x, mod, layer, mod_row, w, out_dtype):
    b, s, d = x.shape
    n = w.shape[2]
    tm = min(s, IN_PROJ_ROWS)
    return pl.pallas_call(
        _inproj_kernel,
        grid=(b, s // tm),
        in_specs=[
            pl.BlockSpec((1, tm, d), lambda bi, i: (bi, i, 0)),
            _mod_spec(d, layer, mod_row, 0),
            _mod_spec(d, layer, mod_row, 1),
            pl.BlockSpec((1, d, n), lambda bi, i: (layer, 0, 0), pipeline_mode=pl.Buffered(1)),
        ],
        out_specs=pl.BlockSpec((1, tm, n), lambda bi, i: (bi, i, 0)),
        out_shape=jax.ShapeDtypeStruct((b, s, n), out_dtype),
        scratch_shapes=[pltpu.VMEM((d, n), BF16)],
        compiler_params=_cparams(("arbitrary", "arbitrary")),
        name="in_projection",
    )(x, mod, mod, w)


def _outproj_kernel(alpha, a_ref, z_ref, x_ref, gate_ref, w_ref, g_ref, b_ref, o_ref, w_bf):
    @pl.when(_first_step())
    def _():
        w_bf[...] = w_ref[0].astype(BF16)

    a = a_ref[0].astype(F32) * _silu(z_ref[0].astype(F32))
    y = _dot(a.astype(BF16), w_bf[...])
    r = alpha * x_ref[0] + gate_ref[0, 0] * y
    mu = jnp.mean(r, axis=-1, keepdims=True)
    dlt = r - mu
    var = jnp.mean(dlt * dlt, axis=-1, keepdims=True)
    o_ref[0] = dlt * lax.rsqrt(var + LN_EPS) * g_ref[0] + b_ref[0]


def _out_projection(a, proj, x, mod, layer, mod_row, w, ln_g, ln_b, alpha):
    b, s, d = x.shape
    tm = min(s, OUT_PROJ_ROWS)
    row = lambda bi, i: (bi, i, 0)
    per_layer = lambda bi, i: (layer, 0, 0)
    return pl.pallas_call(
        functools.partial(_outproj_kernel, alpha),
        grid=(b, s // tm),
        in_specs=[
            pl.BlockSpec((1, tm, d), row),
            pl.BlockSpec((1, tm, d), lambda bi, i: (bi, i, 3)),
            pl.BlockSpec((1, tm, d), row),
            _mod_spec(d, layer, mod_row, 2),
            pl.BlockSpec((1, d, d), per_layer),
            pl.BlockSpec((1, 1, d), per_layer),
            pl.BlockSpec((1, 1, d), per_layer),
        ],
        out_specs=pl.BlockSpec((1, tm, d), row),
        out_shape=jax.ShapeDtypeStruct((b, s, d), F32),
        scratch_shapes=[pltpu.VMEM((d, d), BF16)],
        compiler_params=_cparams(("arbitrary", "arbitrary")),
        name="out_projection",
    )(a, proj, x, mod, w, ln_g, ln_b)


def _col_layout():
    q_cols = np.arange(GRID_W).reshape(N_COL_BLOCKS, Q_BLOCK_W)
    q_start = np.clip(q_cols - KW // 2, 0, GRID_W - KW)
    blk_start = np.clip(np.arange(N_COL_BLOCKS) * Q_BLOCK_W - KW // 2, 0, GRID_W - K_BLOCK_W)
    k_cols = blk_start[:, None] + np.arange(K_BLOCK_W)
    kc = k_cols[:, None, :]
    in_win = (kc >= q_start[:, :, None]) & (kc < q_start[:, :, None] + KW)
    dcol = np.clip(kc - q_cols[:, :, None] + KW - 1, 0, RPB_COLS - 1)
    return blk_start, in_win, dcol


def _window_base(rb, rows):
    return np.clip(rb * Q_ROWS - KH // 2, 0, rows - K_ROWS)


def _na_mask_table(rows):
    _, in_win, _ = _col_layout()
    n_rb = rows // Q_ROWS
    out = np.zeros((3, N_COL_BLOCKS, Q_ROWS, K_ROWS // ROWS_PER_LANE_TILE, Q_BLOCK_W, LANES), np.float32)
    for cls, rb in enumerate((0, 1, n_rb - 1)):
        kr0 = _window_base(rb, rows)
        for qr in range(Q_ROWS):
            r = rb * Q_ROWS + qr
            r0 = np.clip(r - KH // 2, 0, rows - KH)
            for kr in range(K_ROWS):
                row_ok = r0 <= kr0 + kr < r0 + KH
                g, j = divmod(kr, ROWS_PER_LANE_TILE)
                ok = in_win & row_ok
                out[cls, :, qr, g, :, j * K_BLOCK_W:(j + 1) * K_BLOCK_W] = np.where(ok, 0.0, NEG_INF)
    return out


def _rpb_expand_kernel(r_ref, e_ref, o_ref):
    r = r_ref[...]
    hi = r.astype(BF16)
    r1 = r - hi.astype(F32)
    mid = r1.astype(BF16)
    lo = (r1 - mid.astype(F32)).astype(BF16)
    e = e_ref[0]
    o_ref[0] = (_dot(hi, e) + _dot(mid, e) + _dot(lo, e)) * LOG2E


def _na_bias_table(rpb):
    h = rpb.shape[0]
    _, _, dcol = _col_layout()
    kpad = 32
    nj = ROWS_PER_LANE_TILE
    sel = dcol[:, None, :, :] == np.arange(kpad)[None, :, None, None]
    onehot = np.zeros((N_COL_BLOCKS, nj, kpad, Q_BLOCK_W, nj, K_BLOCK_W), np.float32)
    for j in range(nj):
        onehot[:, j, :, :, j, :] = sel
    onehot = onehot.reshape(N_COL_BLOCKS, nj * kpad, Q_BLOCK_W * LANES)
    hi_pad = N_DSTART + nj - 1 - DSTART_SHIFT - RPB_ROWS
    rp = jnp.pad(rpb, ((0, 0), (DSTART_SHIFT, hi_pad), (0, kpad - RPB_COLS)))
    shifted = jnp.concatenate([rp[:, j:j + N_DSTART] for j in range(nj)], axis=-1).reshape(h * N_DSTART, nj * kpad)
    t = pl.pallas_call(
        _rpb_expand_kernel,
        grid=(N_COL_BLOCKS,),
        in_specs=[
            pl.BlockSpec(shifted.shape, lambda n: (0, 0)),
            pl.BlockSpec((1, nj * kpad, Q_BLOCK_W * LANES), lambda n: (n, 0, 0)),
        ],
        out_specs=pl.BlockSpec((1, h * N_DSTART, Q_BLOCK_W * LANES), lambda n: (n, 0, 0)),
        out_shape=jax.ShapeDtypeStruct((N_COL_BLOCKS, h * N_DSTART, Q_BLOCK_W * LANES), F32),
        compiler_params=_cparams(("arbitrary",)),
        name="rpb_expand",
    )(shifted, jnp.asarray(onehot, BF16))
    return t.reshape(N_COL_BLOCKS, h // HEADS_PER_TILE, HEADS_PER_TILE, N_DSTART, Q_BLOCK_W, LANES)


def _na_kernel(rows, blocks_per_step, blk_start, q_ref, k_ref, v_ref, kc_ref, vc_ref, t_ref, m_ref, o_ref):
    n_rb = rows // Q_ROWS
    lane = lax.broadcasted_iota(jnp.int32, (1, LANES), 1)
    kctx = kc_ref[0].astype(BF16)
    vctx = vc_ref[0].astype(BF16)
    n_groups = K_ROWS // ROWS_PER_LANE_TILE
    nq = Q_ROWS * Q_BLOCK_W
    rows_n = HEADS_PER_TILE * nq

    def prepare(sub):
        rb = pl.program_id(2) * blocks_per_step + sub
        kr0 = jnp.clip(rb * Q_ROWS - KH // 2, 0, rows - K_ROWS)
        cls = jnp.where(rb == 0, 0, jnp.where(rb == n_rb - 1, 2, 1))
        off = kr0 - rb * Q_ROWS + KH - 1 + DSTART_SHIFT
        q0 = sub * Q_ROWS * GRID_W
        q2s = []
        for n in range(N_COL_BLOCKS):
            qn = jnp.concatenate(
                [q_ref[0, q0 + qr * GRID_W + n * Q_BLOCK_W: q0 + qr * GRID_W + (n + 1) * Q_BLOCK_W, :]
                 for qr in range(Q_ROWS)], axis=0) * (HEAD_DIM ** -0.5 * LOG2E)
            q2s.append(jnp.concatenate(
                [jnp.where((lane >= h * HEAD_DIM) & (lane < (h + 1) * HEAD_DIM), qn, 0.0)
                 for h in range(HEADS_PER_TILE)], axis=0).astype(BF16))
        s_ctx_all = _dot_nt(jnp.concatenate(q2s, axis=0), kctx)
        return kr0, cls, off, q0, q2s, s_ctx_all

    def scores(prep, n):
        kr0, cls, off, _, q2s, s_ctx_all = prep
        c0 = int(blk_start[n])
        window = lambda ref: jnp.concatenate(
            [ref[0, pl.ds(pl.multiple_of((kr0 + kr) * GRID_W + c0, SUBLANES), K_BLOCK_W), :] for kr in range(K_ROWS)],
            axis=0).astype(BF16)
        bias = jnp.concatenate([
            jnp.concatenate([t_ref[n, 0, h, off + ROWS_PER_LANE_TILE * g - qr] + m_ref[cls, n, qr, g]
                             for g in range(n_groups)], axis=1)
            for h in range(HEADS_PER_TILE) for qr in range(Q_ROWS)], axis=0)
        return _dot_nt(q2s[n], window(k_ref)) + bias, s_ctx_all[n * rows_n:(n + 1) * rows_n], window(v_ref)

    def finish(q0, n, s_lat, s_ctx, vn):
        m = jnp.maximum(jnp.max(s_lat, axis=-1, keepdims=True), jnp.max(s_ctx, axis=-1, keepdims=True))
        e_lat = jnp.exp2(s_lat - m)
        e_ctx = jnp.exp2(s_ctx - m)
        den = jnp.sum(e_lat, axis=-1, keepdims=True) + jnp.sum(e_ctx, axis=-1, keepdims=True)
        o2 = (_dot(e_lat.astype(BF16), vn) + _dot(e_ctx.astype(BF16), vctx)) / den
        o_n = o2[0:nq]
        for h in range(1, HEADS_PER_TILE):
            o_n = jnp.where((lane >= h * HEAD_DIM) & (lane < (h + 1) * HEAD_DIM), o2[h * nq:(h + 1) * nq], o_n)
        for qr in range(Q_ROWS):
            o_ref[0, q0 + qr * GRID_W + n * Q_BLOCK_W: q0 + qr * GRID_W + (n + 1) * Q_BLOCK_W, :] = (
                o_n[qr * Q_BLOCK_W:(qr + 1) * Q_BLOCK_W].astype(o_ref.dtype))

    order = [(sub, n) for sub in range(blocks_per_step) for n in range(N_COL_BLOCKS)]
    preps = {0: prepare(0)}
    cur = scores(preps[0], 0)
    for idx, (sub, n) in enumerate(order):
        nxt = None
        if idx + 1 < len(order):
            nsub, nn = order[idx + 1]
            if nsub not in preps:
                preps[nsub] = prepare(nsub)
            nxt = scores(preps[nsub], nn)
        finish(preps[sub][3], n, *cur)
        cur = nxt


def _neighbourhood_attention(proj, proj_c, bias_table, mask_table):
    b, s, d4 = proj.shape
    d = d4 // 4
    c = proj_c.shape[1]
    rows = s // GRID_W
    n_tiles = d // LANES
    blk_start, _, _ = _col_layout()
    blocks_per_step = min(NA_BLOCKS_PER_STEP, rows // Q_ROWS)
    tq = blocks_per_step * Q_ROWS * GRID_W
    return pl.pallas_call(
        functools.partial(_na_kernel, rows, blocks_per_step, blk_start),
        grid=(b, n_tiles, s // tq),
        in_specs=[
            pl.BlockSpec((1, tq, LANES), lambda bi, hp, rb: (bi, rb, hp)),
            pl.BlockSpec((1, s, LANES), lambda bi, hp, rb: (bi, 0, n_tiles + hp)),
            pl.BlockSpec((1, s, LANES), lambda bi, hp, rb: (bi, 0, 2 * n_tiles + hp)),
            pl.BlockSpec((1, c, LANES), lambda bi, hp, rb: (bi, 0, n_tiles + hp)),
            pl.BlockSpec((1, c, LANES), lambda bi, hp, rb: (bi, 0, 2 * n_tiles + hp)),
            pl.BlockSpec(bias_table.shape[:1] + (1,) + bias_table.shape[2:], lambda bi, hp, rb: (0, hp, 0, 0, 0, 0)),
            pl.BlockSpec(mask_table.shape, lambda bi, hp, rb: (0, 0, 0, 0, 0, 0)),
        ],
        out_specs=pl.BlockSpec((1, tq, LANES), lambda bi, hp, rb: (bi, rb, hp)),
        out_shape=jax.ShapeDtypeStruct((b, s, d), BF16),
        compiler_params=_cparams(("arbitrary", "arbitrary", "arbitrary")),
        name="neighbourhood_attention",
    )(proj, proj, proj, proj_c, proj_c, bias_table, mask_table)


def _ctx_attn_kernel(q_ref, k_ref, v_ref, o_ref):
    lane = lax.broadcasted_iota(jnp.int32, (1, LANES), 1)
    for t in range(q_ref.shape[-1] // LANES):
        cols = slice(t * LANES, (t + 1) * LANES)
        q = q_ref[0, :, cols] * (HEAD_DIM ** -0.5 * LOG2E)
        k = k_ref[0, :, cols].astype(BF16)
        v = v_ref[0, :, cols].astype(BF16)
        out = None
        for h in range(HEADS_PER_TILE):
            in_head = (lane >= h * HEAD_DIM) & (lane < (h + 1) * HEAD_DIM)
            s = _dot_nt(jnp.where(in_head, q, 0.0).astype(BF16), k)
            e = jnp.exp2(s - jnp.max(s, axis=-1, keepdims=True))
            o_h = _dot(e.astype(BF16), v) / jnp.sum(e, axis=-1, keepdims=True)
            out = o_h if out is None else jnp.where(in_head, o_h, out)
        o_ref[0, :, cols] = out.astype(o_ref.dtype)


def _context_attention(proj_c):
    b, c, d4 = proj_c.shape
    d = d4 // 4
    return pl.pallas_call(
        _ctx_attn_kernel,
        grid=(b,),
        in_specs=[
            pl.BlockSpec((1, c, d), lambda bi: (bi, 0, 0)),
            pl.BlockSpec((1, c, d), lambda bi: (bi, 0, 1)),
            pl.BlockSpec((1, c, d), lambda bi: (bi, 0, 2)),
        ],
        out_specs=pl.BlockSpec((1, c, d), lambda bi: (bi, 0, 0)),
        out_shape=jax.ShapeDtypeStruct((b, c, d), BF16),
        compiler_params=_cparams(("arbitrary",)),
        name="context_attention",
    )(proj_c, proj_c, proj_c)


def _conv_block(length):
    return min(CONV_BLOCK, length)


def _dft_matrices(p):
    k = np.arange(p, dtype=np.float64)[:, None]
    n = np.arange(p, dtype=np.float64)[None, :]
    ang = 2.0 * np.pi * k * n / (2 * p)
    f_im = -np.sin(ang)
    f_im[0] = np.cos(np.pi * n[0])
    fwd = np.concatenate([np.cos(ang), f_im], axis=0)
    wk = np.where(k == 0, 1.0, 2.0) / (2 * p)
    ang_i = 2.0 * np.pi * k * (n + p) / (2 * p)
    g_im = -np.sin(ang_i) * wk
    g_im[0] = np.cos(np.pi * (n[0] + p)) / (2 * p)
    inv = np.concatenate([np.cos(ang_i) * wk, g_im], axis=0).T
    return fwd.astype(np.float32), inv.astype(np.float32)


def _filter_kernel(length, p, z_ref, w1_ref, b1_ref, w2_ref, b2_ref, w3_ref, b3_ref, fr_ref, w4_ref,
                   dl_ref, f_ref, o_ref, a_scr, t_scr, prev_scr, norm_scr):
    nb = length // p
    di = pl.program_id(2)
    cb = o_ref.shape[-1]

    @pl.when((pl.program_id(0) == 0) & (pl.program_id(1) == 0) & (di == 0))
    def _():
        a = jnp.sin(fr_ref[0:1, :] * (_dot3(z_ref[...], w1_ref[...]) + b1_ref[...]))
        a = jnp.sin(fr_ref[1:2, :] * (_dot3(a, w2_ref[...]) + b2_ref[...]))
        a_scr[...] = jnp.sin(fr_ref[2:3, :] * (_dot3(a, w3_ref[...]) + b3_ref[...]))

    def block_dft(blk):
        start = pl.multiple_of(blk * p, p)
        return _dot(f_ref[...], t_scr[pl.ds(start, p), :].astype(BF16))

    @pl.when(di == 0)
    def _():
        total = jnp.zeros((1, cb), F32)
        for blk in range(2 * nb):
            back = blk < nb
            idx = lax.broadcasted_iota(jnp.int32, (p, cb), 0) + (blk % nb) * p
            t = ((length - idx) if back else idx).astype(F32) * (1.0 / (length - 1))
            rows_a = (blk % nb) * p
            taps = _dot3(a_scr[rows_a:rows_a + p, :], w4_ref[0, 0 if back else 1]) * jnp.exp(-t * dl_ref[...])
            if blk == 0:
                taps = jnp.where(idx == 0, 0.0, taps)
            t_scr[blk * p:(blk + 1) * p, :] = taps
            total = total + jnp.sum(jnp.abs(taps), axis=0, keepdims=True)
        norm_scr[...] = jnp.broadcast_to(1.0 / total, norm_scr.shape)
        prev_scr[...] = block_dft(0)

    nxt = block_dft(di + 1)
    krow = lax.broadcasted_iota(jnp.int32, (2 * p, cb), 0)
    o_ref[0, 0] = ((prev_scr[...] + jnp.where((krow & 1) == 0, nxt, -nxt)) * norm_scr[0:1, :]).astype(o_ref.dtype)
    prev_scr[...] = nxt


def _hyena_filters(length, w1, b1, w2, b2, w3, b3, w4, freq):
    p = _conv_block(length)
    nb = length // p
    width = w4.shape[-1] // 4
    cb = min(FILTER_LANES, width)
    t = jnp.linspace(0.0, 1.0, length, dtype=F32)[:, None]
    bands = (HY_EMB_DIM - 1) // 2
    wpos = 2.0 * math.pi * jnp.arange(length, dtype=F32)[:, None] / length
    f = jnp.linspace(1e-4, bands - 1, bands, dtype=F32)[None, :]
    z = jnp.concatenate([t, jnp.cos(f * wpos), -jnp.sin(f * wpos)], axis=-1)
    z_back = jnp.concatenate([jnp.zeros((1, HY_EMB_DIM), F32), z[1:][::-1]], axis=0)
    hid = w2.shape[0]
    emb_pad = ((0, 0), (0, hid - HY_EMB_DIM))
    z_all = jnp.concatenate([jnp.pad(z_back, emb_pad), jnp.pad(z, emb_pad)], axis=1)
    both = lambda w: jnp.kron(jnp.eye(2, dtype=F32), w)
    twice = lambda v: jnp.tile(v.reshape(-1, hid), (1, 2))
    w1p = both(jnp.pad(w1, ((0, hid - HY_EMB_DIM), (0, 0))))
    w2, w3 = both(w2), both(w3)
    b1, b2, b3, freq = twice(b1), twice(b2), twice(b3), twice(freq)
    order = w4.shape[-1] // (2 * width)
    w4r = jnp.transpose(w4.reshape(w4.shape[0], order, 2, width), (1, 2, 0, 3))
    zero = jnp.zeros_like(w4r[:, 0])
    w4r = jnp.stack([jnp.concatenate([w4r[:, 1], zero], axis=1),
                     jnp.concatenate([zero, w4r[:, 0]], axis=1)], axis=1)
    hid = 2 * hid
    max_decay = math.log(HY_TARGET) / HY_FAST_DECAY
    min_decay = math.log(HY_TARGET) / HY_SLOW_DECAY
    deltas = jnp.abs(jnp.linspace(min_decay, max_decay, width, dtype=F32)).reshape(1, width)
    fwd = jnp.asarray(_dft_matrices(p)[0]).astype(BF16)
    full = lambda shape: pl.BlockSpec(shape, lambda o, c, di: (0,) * len(shape))
    return pl.pallas_call(
        functools.partial(_filter_kernel, length, p),
        grid=(order, width // cb, 2 * nb - 1),
        in_specs=[
            full(z_all.shape), full(w1p.shape), full((1, hid)), full(w2.shape), full((1, hid)),
            full(w3.shape), full((1, hid)), full(freq.shape),
            pl.BlockSpec((1, 2, hid, cb), lambda o, c, di: (o, 0, 0, c)),
            pl.BlockSpec((1, cb), lambda o, c, di: (0, c)),
            full(fwd.shape),
        ],
        out_specs=pl.BlockSpec((1, 1, 2 * p, cb), lambda o, c, di: (o, di, 0, c)),
        out_shape=jax.ShapeDtypeStruct((order, 2 * nb - 1, 2 * p, width), BF16),
        scratch_shapes=[pltpu.VMEM((length, hid), F32), pltpu.VMEM((2 * length, cb), F32),
                        pltpu.VMEM((2 * p, cb), F32), pltpu.VMEM((SUBLANES, cb), F32)],
        compiler_params=_cparams(("arbitrary", "arbitrary", "arbitrary")),
        name="hyena_filters",
    )(z_all, w1p, b1, w2, b2, w3, b3, freq, w4r, deltas, fwd)


def _short_conv_block(pad_ref, j, p, w_ref, b_ref):
    base = SUBLANES + j * p
    return (pad_ref[base - 1: base - 1 + p, :] * w_ref[0:1, :] + pad_ref[base: base + p, :] * w_ref[1:2, :]
            + pad_ref[base + 1: base + 1 + p, :] * w_ref[2:3, :] + b_ref[...])


def _zero_halo(pad_ref, length):
    zeros = jnp.zeros((SUBLANES, pad_ref.shape[1]), F32)
    pad_ref[0:SUBLANES, :] = zeros
    pad_ref[SUBLANES + length: 2 * SUBLANES + length, :] = zeros


def _fill_rows(pad_ref, src_ref, start, stop):
    pad_ref[SUBLANES + start: SUBLANES + stop, :] = src_ref[start:stop, :].astype(F32)


def _conv_kernel(length, p, conv_u, u_ref, m_ref, h_ref, f_ref, g_ref, cwu_ref, cbu_ref, cwm_ref, cbm_ref,
                 skip_ref, o_ref, pad, uconv, uhat, yhat):
    nb = length // p
    nbat = u_ref.shape[0]
    lanes = lambda bi: slice(bi * LANES, (bi + 1) * LANES)

    for bi in range(nbat):
        _zero_halo(pad.at[bi], length)
        if conv_u:
            _fill_rows(pad.at[bi], u_ref.at[bi], 0, p)
    fwd = f_ref[...]
    for j in range(nb):
        blocks = []
        for bi in range(nbat):
            if conv_u:
                if j + 1 < nb:
                    _fill_rows(pad.at[bi], u_ref.at[bi], (j + 1) * p, (j + 2) * p)
                uj = _short_conv_block(pad.at[bi], j, p, cwu_ref, cbu_ref)
                uconv[bi, j * p:(j + 1) * p, :] = uj
            else:
                uj = u_ref[bi, j * p:(j + 1) * p, :]
            blocks.append(uj.astype(BF16))
        uhat[j] = _dot(fwd, jnp.concatenate(blocks, axis=1))
    for bi in range(nbat):
        _fill_rows(pad.at[bi], m_ref.at[bi], 0, p)

    inv = g_ref[...]
    first = lax.broadcasted_iota(jnp.int32, (SUBLANES, nbat * LANES), 0) == 0
    for i in range(nb):
        for c in range(p // ACC_ROWS):
            ra = slice(c * ACC_ROWS, (c + 1) * ACC_ROWS)
            rb = slice(p + c * ACC_ROWS, p + (c + 1) * ACC_ROWS)
            acc = [[jnp.zeros((ACC_ROWS, LANES), F32) for _ in range(2)] for _ in range(nbat)]
            for j in range(nb):
                h_d = h_ref.at[0, i - j + nb - 1]
                ha, hb = h_d[ra, :].astype(F32), h_d[rb, :].astype(F32)
                for bi in range(nbat):
                    ua, ub = uhat[j, ra, lanes(bi)], uhat[j, rb, lanes(bi)]
                    acc[bi][0] = acc[bi][0] + (ha * ua - hb * ub)
                    acc[bi][1] = acc[bi][1] + (ha * ub + hb * ua)
            for bi in range(nbat):
                yhat[ra, lanes(bi)] = acc[bi][0]
                yhat[rb, lanes(bi)] = acc[bi][1]
        dc = jnp.zeros((SUBLANES, nbat * LANES), F32)
        ny = jnp.zeros((SUBLANES, nbat * LANES), F32)
        for j in range(nb):
            h_d = h_ref.at[0, i - j + nb - 1]
            h_dc = h_d[0:2 * SUBLANES, :].astype(F32)[0:SUBLANES]
            h_ny = h_d[p:p + 2 * SUBLANES, :].astype(F32)[0:SUBLANES]
            dc = dc + jnp.concatenate([h_dc] * nbat, axis=1) * uhat[j, 0:SUBLANES, :]
            ny = ny + jnp.concatenate([h_ny] * nbat, axis=1) * uhat[j, p:p + SUBLANES, :]
        yhat[0:SUBLANES, :] = jnp.where(first, dc, yhat[0:SUBLANES, :])
        yhat[p:p + SUBLANES, :] = jnp.where(first, ny, yhat[p:p + SUBLANES, :])

        y = _dot(inv, yhat[...].astype(BF16))
        for bi in range(nbat):
            if i + 1 < nb:
                _fill_rows(pad.at[bi], m_ref.at[bi], (i + 1) * p, (i + 2) * p)
            mi = _short_conv_block(pad.at[bi], i, p, cwm_ref, cbm_ref)
            src = uconv if conv_u else u_ref
            yi = y[:, lanes(bi)] + src[bi, i * p:(i + 1) * p, :].astype(F32) * skip_ref[...]
            o_ref[bi, i * p:(i + 1) * p, :] = (mi * yi).astype(o_ref.dtype)


def _long_conv_gate(u_src, u_col, conv_u, proj, m_col, spectra, order, conv_w, conv_b, skip, out_dtype):
    b, length, _ = proj.shape
    width = skip.shape[-1]
    p = _conv_block(length)
    nb = length // p
    cb = LANES
    ncb = width // cb
    nbat = CONV_BATCH if nb > 1 else b
    assert b % nbat == 0 and p % ACC_ROWS == 0
    fwd_np, inv_np = _dft_matrices(p)
    fwd = jnp.asarray(fwd_np).astype(BF16)
    inv = jnp.asarray(inv_np).astype(BF16)
    col = lambda base: (lambda c, bi: (bi, 0, base * ncb + c))
    par = lambda base: (lambda c, bi: (0, base * ncb + c))
    once = pl.Buffered(1)
    return pl.pallas_call(
        functools.partial(_conv_kernel, length, p, conv_u),
        grid=(ncb, b // nbat),
        in_specs=[
            pl.BlockSpec((nbat, length, cb), col(u_col)),
            pl.BlockSpec((nbat, length, cb), col(m_col)),
            pl.BlockSpec((1, 2 * nb - 1, 2 * p, cb), lambda c, bi: (order, 0, 0, c)),
            pl.BlockSpec(fwd.shape, lambda c, bi: (0, 0), pipeline_mode=once),
            pl.BlockSpec(inv.shape, lambda c, bi: (0, 0), pipeline_mode=once),
            pl.BlockSpec((3, cb), par(0)),
            pl.BlockSpec((1, cb), par(0)),
            pl.BlockSpec((3, cb), par(m_col)),
            pl.BlockSpec((1, cb), par(m_col)),
            pl.BlockSpec((1, cb), lambda c, bi: (0, c)),
        ],
        out_specs=pl.BlockSpec((nbat, length, cb), lambda c, bi: (bi, 0, c)),
        out_shape=jax.ShapeDtypeStruct((b, length, width), out_dtype),
        scratch_shapes=[
            pltpu.VMEM((nbat, length + 2 * SUBLANES, cb), F32),
            pltpu.VMEM((nbat, length, cb) if conv_u else (nbat, SUBLANES, cb), F32),
            pltpu.VMEM((nb, 2 * p, nbat * cb), F32), pltpu.VMEM((2 * p, nbat * cb), F32),
        ],
        compiler_params=_cparams(("arbitrary", "arbitrary")),
        name="long_conv_gate",
    )(u_src, proj, spectra, fwd, inv, conv_w, conv_b.reshape(1, -1), conv_w, conv_b.reshape(1, -1),
      skip[order].reshape(1, width))


def _hyena_core(proj, spectra, conv_w, conv_b, skip):
    z = _long_conv_gate(proj, 0, True, proj, 1, spectra, 0, conv_w, conv_b, skip, F32)
    return _long_conv_gate(z, 0, False, proj, 2, spectra, 1, conv_w, conv_b, skip, BF16)


def kernel(x, c, ctx, c_ctx, w_ada, b_ada, w_in, w_out, ln_g, ln_b, na_rpb, hy_conv_w, hy_conv_b, hy_f_w1, hy_f_b1,
           hy_f_w2, hy_f_b2, hy_f_w3, hy_f_b3, hy_f_w4, hy_f_freq, hy_skip):
    depth = w_in.shape[0]
    b, s, d = x.shape
    n_ctx = ctx.shape[1]
    n_mixers = 2
    alpha = (2 * depth) ** 0.25

    cc = jnp.concatenate([c, c_ctx[None, :], jnp.zeros((2 * SUBLANES - b - 1, d), F32)], axis=0)
    mod = _ada_vectors(cc, w_ada, b_ada)
    mod = mod.reshape(depth, mod.shape[1], 1, 3 * d)
    ln_g = ln_g.reshape(depth, 1, d)
    ln_b = ln_b.reshape(depth, 1, d)
    mask_table = jnp.asarray(_na_mask_table(s // GRID_W))

    for i in range(depth):
        last = i == depth - 1
        j = i // n_mixers
        proj_dtype = F32 if i % n_mixers == 0 else BF16
        proj = _in_projection(x, mod, i, None, w_in, proj_dtype)
        proj_c = None
        if i % n_mixers == 0 or not last:
            proj_c = _in_projection(ctx, mod, i, b, w_in, proj_dtype)
        if i % n_mixers == 0:
            bias_table = _na_bias_table(na_rpb[j])
            a = _neighbourhood_attention(proj, proj_c, bias_table, mask_table)
            a_c = None if last else _context_attention(proj_c)
        else:
            filt = (hy_f_w1[j], hy_f_b1[j], hy_f_w2[j], hy_f_b2[j], hy_f_w3[j], hy_f_b3[j], hy_f_w4[j], hy_f_freq[j])
            a = _hyena_core(proj, _hyena_filters(s, *filt), hy_conv_w[j], hy_conv_b[j], hy_skip[j])
            a_c = None if last else _hyena_core(proj_c, _hyena_filters(n_ctx, *filt), hy_conv_w[j], hy_conv_b[j],
                                                hy_skip[j])
        x = _out_projection(a, proj, x, mod, i, None, w_out, ln_g, ln_b, alpha)
        if not last:
            ctx = _out_projection(a_c, proj_c, ctx, mod, i, b, w_out, ln_g, ln_b, alpha)
    return x
```

```python
import functools
import math

import numpy as np
import jax
import jax.numpy as jnp
from jax import lax
from jax.experimental import pallas as pl
from jax.experimental.pallas import tpu as pltpu

F32 = jnp.float32
BF16 = jnp.bfloat16

HEAD_DIM = 64
GRID_W = 64
KH = 8
KW = 16
Q_BLOCK_W = 16
K_BLOCK_W = Q_BLOCK_W + KW
N_COL_BLOCKS = GRID_W // Q_BLOCK_W
RPB_ROWS = 2 * KH - 1
RPB_COLS = 2 * KW - 1
HY_EMB_DIM = 33
HY_FAST_DECAY = 0.3
HY_SLOW_DECAY = 1.5
HY_TARGET = 1e-2
LN_EPS = 1e-5
NEG_INF = -1e30
LOG2E = 1.4426950408889634

LANES = 128
SUBLANES = 8
HEADS_PER_TILE = LANES // HEAD_DIM
VMEM_LIMIT = 56 * 1024 * 1024

Q_ROWS = 8
NA_BLOCKS_PER_STEP = 8
OUT_PROJ_ROWS = 1024
K_ROWS = 16
ROWS_PER_LANE_TILE = LANES // K_BLOCK_W
N_DSTART = 28
DSTART_SHIFT = 8
CONV_BLOCK = 1024
CONV_BATCH = 2
ACC_ROWS = 32
FILTER_LANES = 256
IN_PROJ_ROWS = 512


def _cparams(sem):
    return pltpu.CompilerParams(dimension_semantics=sem, vmem_limit_bytes=VMEM_LIMIT)


def _split_bf16(a):
    hi = a.astype(BF16)
    lo = (a - hi.astype(F32)).astype(BF16)
    return hi, lo


def _dot(a, b):
    return jnp.dot(a, b, preferred_element_type=F32)


def _dot_nt(a, b):
    return lax.dot_general(a, b, (((1,), (1,)), ((), ())), preferred_element_type=F32)


def _dot3(a, b):
    ah, al = _split_bf16(a)
    bh, bl = _split_bf16(b)
    return _dot(ah, bh) + _dot(al, bh) + _dot(ah, bl)


def _silu(x):
    return x * jax.nn.sigmoid(x)


def _ada_kernel(cc_ref, w_ref, b_ref, o_ref):
    o_ref[0] = _dot3(_silu(cc_ref[...]), w_ref[0]) + b_ref[0]


def _ada_vectors(cc, w_ada, b_ada):
    depth, d, n = w_ada.shape
    tn = min(n, 1024)
    return pl.pallas_call(
        _ada_kernel,
        grid=(depth, n // tn),
        in_specs=[
            pl.BlockSpec(cc.shape, lambda i, j: (0, 0)),
            pl.BlockSpec((1, d, tn), lambda i, j: (i, 0, j)),
            pl.BlockSpec((1, 1, tn), lambda i, j: (i, 0, j)),
        ],
        out_specs=pl.BlockSpec((1, cc.shape[0], tn), lambda i, j: (i, 0, j)),
        out_shape=jax.ShapeDtypeStruct((depth, cc.shape[0], n), F32),
        compiler_params=_cparams(("arbitrary", "arbitrary")),
        name="ada_vectors",
    )(cc, w_ada, b_ada.reshape(depth, 1, n))


def _mod_spec(d, layer, mod_row, part):
    return pl.BlockSpec((1, 1, 1, d), lambda bi, i: (layer, bi if mod_row is None else mod_row, 0, part))


def _first_step():
    return (pl.program_id(0) == 0) & (pl.program_id(1) == 0)


def _inproj_kernel(x_ref, sh_ref, sc_ref, w_ref, o_ref, gate_ref, w_bf):
    @pl.when(_first_step())
    def _():
        w_bf[...] = w_ref[0].astype(BF16)

    h = (x_ref[0] * (1.0 + sc_ref[0, 0]) + sh_ref[0, 0]).astype(BF16)
    n_main = o_ref.shape[-1]
    o_ref[0] = _dot(h, w_bf[:, 0:n_main]).astype(o_ref.dtype)
    if gate_ref:
        gate_ref[0][0] = _dot(h, w_bf[:, n_main:]).astype(gate_ref[0].dtype)


def _in_projection(x, mod, layer, mod_row, w, out_dtype, split_gate=False):
    b, s, d = x.shape
    n = w.shape[2]
    n_main = n - d if split_gate else n
    tm = min(s, IN_PROJ_ROWS * (F32.dtype.itemsize // jnp.dtype(out_dtype).itemsize))
    out_specs = [pl.BlockSpec((1, tm, n_main), lambda bi, i: (bi, i, 0))]
    out_shape = [jax.ShapeDtypeStruct((b, s, n_main), out_dtype)]
    if split_gate:
        out_specs.append(pl.BlockSpec((1, tm, d), lambda bi, i: (bi, i, 0)))
        out_shape.append(jax.ShapeDtypeStruct((b, s, d), BF16))

    def body(x_ref, sh_ref, sc_ref, w_ref, o_ref, *rest):
        _inproj_kernel(x_ref, sh_ref, sc_ref, w_ref, o_ref, rest[:-1], rest[-1])

    out = pl.pallas_call(
        body,
        grid=(b, s // tm),
        in_specs=[
            pl.BlockSpec((1, tm, d), lambda bi, i: (bi, i, 0)),
            _mod_spec(d, layer, mod_row, 0),
            _mod_spec(d, layer, mod_row, 1),
            pl.BlockSpec((1, d, n), lambda bi, i: (layer, 0, 0), pipeline_mode=pl.Buffered(1)),
        ],
        out_specs=out_specs,
        out_shape=out_shape,
        scratch_shapes=[pltpu.VMEM((d, n), BF16)],
        compiler_params=_cparams(("arbitrary", "arbitrary")),
        name="in_projection",
    )(x, mod, mod, w)
    return (out[0], out[1]) if split_gate else (out[0], None)


def _outproj_kernel(alpha, a_ref, z_ref, x_ref, gate_ref, w_ref, g_ref, b_ref, o_ref, w_bf):
    @pl.when(_first_step())
    def _():
        w_bf[...] = w_ref[0].astype(BF16)

    a = a_ref[0].astype(F32) * _silu(z_ref[0].astype(F32))
    y = _dot(a.astype(BF16), w_bf[...])
    r = alpha * x_ref[0] + gate_ref[0, 0] * y
    mu = jnp.mean(r, axis=-1, keepdims=True)
    dlt = r - mu
    var = jnp.mean(dlt * dlt, axis=-1, keepdims=True)
    o_ref[0] = dlt * lax.rsqrt(var + LN_EPS) * g_ref[0] + b_ref[0]


def _out_projection(a, z_src, x, mod, layer, mod_row, w, ln_g, ln_b, alpha):
    b, s, d = x.shape
    z_col = z_src.shape[-1] // d - 1
    tm = min(s, OUT_PROJ_ROWS)
    row = lambda bi, i: (bi, i, 0)
    per_layer = lambda bi, i: (layer, 0, 0)
    return pl.pallas_call(
        functools.partial(_outproj_kernel, alpha),
        grid=(b, s // tm),
        in_specs=[
            pl.BlockSpec((1, tm, d), row),
            pl.BlockSpec((1, tm, d), lambda bi, i: (bi, i, z_col)),
            pl.BlockSpec((1, tm, d), row),
            _mod_spec(d, layer, mod_row, 2),
            pl.BlockSpec((1, d, d), per_layer),
            pl.BlockSpec((1, 1, d), per_layer),
            pl.BlockSpec((1, 1, d), per_layer),
        ],
        out_specs=pl.BlockSpec((1, tm, d), row),
        out_shape=jax.ShapeDtypeStruct((b, s, d), F32),
        scratch_shapes=[pltpu.VMEM((d, d), BF16)],
        compiler_params=_cparams(("arbitrary", "arbitrary")),
        name="out_projection",
    )(a, z_src, x, mod, w, ln_g, ln_b)


def _col_layout():
    q_cols = np.arange(GRID_W).reshape(N_COL_BLOCKS, Q_BLOCK_W)
    q_start = np.clip(q_cols - KW // 2, 0, GRID_W - KW)
    blk_start = np.clip(np.arange(N_COL_BLOCKS) * Q_BLOCK_W - KW // 2, 0, GRID_W - K_BLOCK_W)
    k_cols = blk_start[:, None] + np.arange(K_BLOCK_W)
    kc = k_cols[:, None, :]
    in_win = (kc >= q_start[:, :, None]) & (kc < q_start[:, :, None] + KW)
    dcol = np.clip(kc - q_cols[:, :, None] + KW - 1, 0, RPB_COLS - 1)
    return blk_start, in_win, dcol


def _window_base(rb, rows):
    return np.clip(rb * Q_ROWS - KH // 2, 0, rows - K_ROWS)


def _na_mask_table(rows):
    _, in_win, _ = _col_layout()
    n_rb = rows // Q_ROWS
    out = np.zeros((3, N_COL_BLOCKS, Q_ROWS, K_ROWS // ROWS_PER_LANE_TILE, Q_BLOCK_W, LANES), np.float32)
    for cls, rb in enumerate((0, 1, n_rb - 1)):
        kr0 = _window_base(rb, rows)
        for qr in range(Q_ROWS):
            r = rb * Q_ROWS + qr
            r0 = np.clip(r - KH // 2, 0, rows - KH)
            for kr in range(K_ROWS):
                row_ok = r0 <= kr0 + kr < r0 + KH
                g, j = divmod(kr, ROWS_PER_LANE_TILE)
                ok = in_win & row_ok
                out[cls, :, qr, g, :, j * K_BLOCK_W:(j + 1) * K_BLOCK_W] = np.where(ok, 0.0, NEG_INF)
    return out


def _rpb_expand_kernel(r_ref, e_ref, o_ref):
    r = r_ref[...]
    hi = r.astype(BF16)
    r1 = r - hi.astype(F32)
    mid = r1.astype(BF16)
    lo = (r1 - mid.astype(F32)).astype(BF16)
    e = e_ref[0]
    o_ref[0] = (_dot(hi, e) + _dot(mid, e) + _dot(lo, e)) * LOG2E


def _na_bias_table(rpb):
    h = rpb.shape[0]
    _, _, dcol = _col_layout()
    kpad = 32
    nj = ROWS_PER_LANE_TILE
    sel = dcol[:, None, :, :] == np.arange(kpad)[None, :, None, None]
    onehot = np.zeros((N_COL_BLOCKS, nj, kpad, Q_BLOCK_W, nj, K_BLOCK_W), np.float32)
    for j in range(nj):
        onehot[:, j, :, :, j, :] = sel
    onehot = onehot.reshape(N_COL_BLOCKS, nj * kpad, Q_BLOCK_W * LANES)
    hi_pad = N_DSTART + nj - 1 - DSTART_SHIFT - RPB_ROWS
    rp = jnp.pad(rpb, ((0, 0), (DSTART_SHIFT, hi_pad), (0, kpad - RPB_COLS)))
    shifted = jnp.concatenate([rp[:, j:j + N_DSTART] for j in range(nj)], axis=-1).reshape(h * N_DSTART, nj * kpad)
    t = pl.pallas_call(
        _rpb_expand_kernel,
        grid=(N_COL_BLOCKS,),
        in_specs=[
            pl.BlockSpec(shifted.shape, lambda n: (0, 0)),
            pl.BlockSpec((1, nj * kpad, Q_BLOCK_W * LANES), lambda n: (n, 0, 0)),
        ],
        out_specs=pl.BlockSpec((1, h * N_DSTART, Q_BLOCK_W * LANES), lambda n: (n, 0, 0)),
        out_shape=jax.ShapeDtypeStruct((N_COL_BLOCKS, h * N_DSTART, Q_BLOCK_W * LANES), F32),
        compiler_params=_cparams(("arbitrary",)),
        name="rpb_expand",
    )(shifted, jnp.asarray(onehot, BF16))
    return t.reshape(N_COL_BLOCKS, h // HEADS_PER_TILE, HEADS_PER_TILE, N_DSTART, Q_BLOCK_W, LANES)


def _na_kernel(rows, blocks_per_step, blk_start, q_ref, k_ref, v_ref, kc_ref, vc_ref, t_ref, m_ref, o_ref):
    n_rb = rows // Q_ROWS
    lane = lax.broadcasted_iota(jnp.int32, (1, LANES), 1)
    kctx = kc_ref[0].astype(BF16)
    vctx = vc_ref[0].astype(BF16)
    n_groups = K_ROWS // ROWS_PER_LANE_TILE
    nq = Q_ROWS * Q_BLOCK_W
    rows_n = HEADS_PER_TILE * nq

    def prepare(sub):
        rb = pl.program_id(2) * blocks_per_step + sub
        kr0 = jnp.clip(rb * Q_ROWS - KH // 2, 0, rows - K_ROWS)
        cls = jnp.where(rb == 0, 0, jnp.where(rb == n_rb - 1, 2, 1))
        off = kr0 - rb * Q_ROWS + KH - 1 + DSTART_SHIFT
        q0 = sub * Q_ROWS * GRID_W
        q2s = []
        for n in range(N_COL_BLOCKS):
            qn = jnp.concatenate(
                [q_ref[0, q0 + qr * GRID_W + n * Q_BLOCK_W: q0 + qr * GRID_W + (n + 1) * Q_BLOCK_W, :]
                 for qr in range(Q_ROWS)], axis=0) * (HEAD_DIM ** -0.5 * LOG2E)
            q2s.append(jnp.concatenate(
                [jnp.where((lane >= h * HEAD_DIM) & (lane < (h + 1) * HEAD_DIM), qn, 0.0)
                 for h in range(HEADS_PER_TILE)], axis=0).astype(BF16))
        s_ctx_all = _dot_nt(jnp.concatenate(q2s, axis=0), kctx)
        return kr0, cls, off, q0, q2s, s_ctx_all

    def scores(prep, n):
        kr0, cls, off, _, q2s, s_ctx_all = prep
        c0 = int(blk_start[n])
        window = lambda ref: jnp.concatenate(
            [ref[0, pl.ds(pl.multiple_of((kr0 + kr) * GRID_W + c0, SUBLANES), K_BLOCK_W), :] for kr in range(K_ROWS)],
            axis=0).astype(BF16)
        bias = jnp.concatenate([
            jnp.concatenate([t_ref[n, 0, h, off + ROWS_PER_LANE_TILE * g - qr] + m_ref[cls, n, qr, g]
                             for g in range(n_groups)], axis=1)
            for h in range(HEADS_PER_TILE) for qr in range(Q_ROWS)], axis=0)
        return _dot_nt(q2s[n], window(k_ref)) + bias, s_ctx_all[n * rows_n:(n + 1) * rows_n], window(v_ref)

    def finish(q0, n, s_lat, s_ctx, vn):
        m = jnp.maximum(jnp.max(s_lat, axis=-1, keepdims=True), jnp.max(s_ctx, axis=-1, keepdims=True))
        e_lat = jnp.exp2(s_lat - m)
        e_ctx = jnp.exp2(s_ctx - m)
        den = jnp.sum(e_lat, axis=-1, keepdims=True) + jnp.sum(e_ctx, axis=-1, keepdims=True)
        o2 = (_dot(e_lat.astype(BF16), vn) + _dot(e_ctx.astype(BF16), vctx)) / den
        o_n = o2[0:nq]
        for h in range(1, HEADS_PER_TILE):
            o_n = jnp.where((lane >= h * HEAD_DIM) & (lane < (h + 1) * HEAD_DIM), o2[h * nq:(h + 1) * nq], o_n)
        for qr in range(Q_ROWS):
            o_ref[0, q0 + qr * GRID_W + n * Q_BLOCK_W: q0 + qr * GRID_W + (n + 1) * Q_BLOCK_W, :] = (
                o_n[qr * Q_BLOCK_W:(qr + 1) * Q_BLOCK_W].astype(o_ref.dtype))

    order = [(sub, n) for sub in range(blocks_per_step) for n in range(N_COL_BLOCKS)]
    preps = {0: prepare(0)}
    cur = scores(preps[0], 0)
    for idx, (sub, n) in enumerate(order):
        nxt = None
        if idx + 1 < len(order):
            nsub, nn = order[idx + 1]
            if nsub not in preps:
                preps[nsub] = prepare(nsub)
            nxt = scores(preps[nsub], nn)
        finish(preps[sub][3], n, *cur)
        cur = nxt


def _neighbourhood_attention(proj, proj_c, bias_table, mask_table):
    b, s, d3 = proj.shape
    d = d3 // 3
    c = proj_c.shape[1]
    rows = s // GRID_W
    n_tiles = d // LANES
    blk_start, _, _ = _col_layout()
    blocks_per_step = min(NA_BLOCKS_PER_STEP, rows // Q_ROWS)
    tq = blocks_per_step * Q_ROWS * GRID_W
    return pl.pallas_call(
        functools.partial(_na_kernel, rows, blocks_per_step, blk_start),
        grid=(b, n_tiles, s // tq),
        in_specs=[
            pl.BlockSpec((1, tq, LANES), lambda bi, hp, rb: (bi, rb, hp)),
            pl.BlockSpec((1, s, LANES), lambda bi, hp, rb: (bi, 0, n_tiles + hp)),
            pl.BlockSpec((1, s, LANES), lambda bi, hp, rb: (bi, 0, 2 * n_tiles + hp)),
            pl.BlockSpec((1, c, LANES), lambda bi, hp, rb: (bi, 0, n_tiles + hp)),
            pl.BlockSpec((1, c, LANES), lambda bi, hp, rb: (bi, 0, 2 * n_tiles + hp)),
            pl.BlockSpec(bias_table.shape[:1] + (1,) + bias_table.shape[2:], lambda bi, hp, rb: (0, hp, 0, 0, 0, 0)),
            pl.BlockSpec(mask_table.shape, lambda bi, hp, rb: (0, 0, 0, 0, 0, 0)),
        ],
        out_specs=pl.BlockSpec((1, tq, LANES), lambda bi, hp, rb: (bi, rb, hp)),
        out_shape=jax.ShapeDtypeStruct((b, s, d), BF16),
        compiler_params=_cparams(("arbitrary", "arbitrary", "arbitrary")),
        name="neighbourhood_attention",
    )(proj, proj, proj, proj_c, proj_c, bias_table, mask_table)


def _ctx_attn_kernel(q_ref, k_ref, v_ref, o_ref):
    lane = lax.broadcasted_iota(jnp.int32, (1, LANES), 1)
    for t in range(q_ref.shape[-1] // LANES):
        cols = slice(t * LANES, (t + 1) * LANES)
        q = q_ref[0, :, cols] * (HEAD_DIM ** -0.5 * LOG2E)
        k = k_ref[0, :, cols].astype(BF16)
        v = v_ref[0, :, cols].astype(BF16)
        out = None
        for h in range(HEADS_PER_TILE):
            in_head = (lane >= h * HEAD_DIM) & (lane < (h + 1) * HEAD_DIM)
            s = _dot_nt(jnp.where(in_head, q, 0.0).astype(BF16), k)
            e = jnp.exp2(s - jnp.max(s, axis=-1, keepdims=True))
            o_h = _dot(e.astype(BF16), v) / jnp.sum(e, axis=-1, keepdims=True)
            out = o_h if out is None else jnp.where(in_head, o_h, out)
        o_ref[0, :, cols] = out.astype(o_ref.dtype)


def _context_attention(proj_c):
    b, c, d3 = proj_c.shape
    d = d3 // 3
    return pl.pallas_call(
        _ctx_attn_kernel,
        grid=(b,),
        in_specs=[
            pl.BlockSpec((1, c, d), lambda bi: (bi, 0, 0)),
            pl.BlockSpec((1, c, d), lambda bi: (bi, 0, 1)),
            pl.BlockSpec((1, c, d), lambda bi: (bi, 0, 2)),
        ],
        out_specs=pl.BlockSpec((1, c, d), lambda bi: (bi, 0, 0)),
        out_shape=jax.ShapeDtypeStruct((b, c, d), BF16),
        compiler_params=_cparams(("arbitrary",)),
        name="context_attention",
    )(proj_c, proj_c, proj_c)


def _conv_block(length):
    return min(CONV_BLOCK, length)


def _dft_matrices(p):
    k = np.arange(p, dtype=np.float64)[:, None]
    n = np.arange(p, dtype=np.float64)[None, :]
    ang = 2.0 * np.pi * k * n / (2 * p)
    f_im = -np.sin(ang)
    f_im[0] = np.cos(np.pi * n[0])
    fwd = np.concatenate([np.cos(ang), f_im], axis=0)
    wk = np.where(k == 0, 1.0, 2.0) / (2 * p)
    ang_i = 2.0 * np.pi * k * (n + p) / (2 * p)
    g_im = -np.sin(ang_i) * wk
    g_im[0] = np.cos(np.pi * (n[0] + p)) / (2 * p)
    inv = np.concatenate([np.cos(ang_i) * wk, g_im], axis=0).T
    return fwd.astype(np.float32), inv.astype(np.float32)


def _filter_kernel(length, p, z_ref, w1_ref, b1_ref, w2_ref, b2_ref, w3_ref, b3_ref, fr_ref, w4_ref,
                   dl_ref, f_ref, o_ref, a_scr, t_scr, prev_scr, norm_scr):
    nb = length // p
    di = pl.program_id(2)
    cb = o_ref.shape[-1]

    @pl.when((pl.program_id(0) == 0) & (pl.program_id(1) == 0) & (di == 0))
    def _():
        a = jnp.sin(fr_ref[0:1, :] * (_dot3(z_ref[...], w1_ref[...]) + b1_ref[...]))
        a = jnp.sin(fr_ref[1:2, :] * (_dot3(a, w2_ref[...]) + b2_ref[...]))
        a_scr[...] = jnp.sin(fr_ref[2:3, :] * (_dot3(a, w3_ref[...]) + b3_ref[...]))

    def block_dft(blk):
        start = pl.multiple_of(blk * p, p)
        return _dot(f_ref[...], t_scr[pl.ds(start, p), :].astype(BF16))

    @pl.when(di == 0)
    def _():
        total = jnp.zeros((1, cb), F32)
        for blk in range(2 * nb):
            back = blk < nb
            idx = lax.broadcasted_iota(jnp.int32, (p, cb), 0) + (blk % nb) * p
            t = ((length - idx) if back else idx).astype(F32) * (1.0 / (length - 1))
            rows_a = (blk % nb) * p
            taps = (_dot(a_scr[rows_a:rows_a + p, :].astype(BF16), w4_ref[0, 0 if back else 1].astype(BF16))
                    * jnp.exp(-t * dl_ref[...]))
            if blk == 0:
                taps = jnp.where(idx == 0, 0.0, taps)
            t_scr[blk * p:(blk + 1) * p, :] = taps
            total = total + jnp.sum(jnp.abs(taps), axis=0, keepdims=True)
        norm_scr[...] = jnp.broadcast_to(1.0 / total, norm_scr.shape)
        prev_scr[...] = block_dft(0)

    nxt = block_dft(di + 1)
    krow = lax.broadcasted_iota(jnp.int32, (2 * p, cb), 0)
    o_ref[0, 0] = ((prev_scr[...] + jnp.where((krow & 1) == 0, nxt, -nxt)) * norm_scr[0:1, :]).astype(o_ref.dtype)
    prev_scr[...] = nxt


def _position_features(length, hid):
    t = np.linspace(0.0, 1.0, length)[:, None]
    bands = (HY_EMB_DIM - 1) // 2
    wpos = 2.0 * np.pi * np.arange(length)[:, None] / length
    f = np.linspace(1e-4, bands - 1, bands)[None, :]
    z = np.concatenate([t, np.cos(f * wpos), -np.sin(f * wpos)], axis=-1)
    out = np.zeros((length, 2 * hid), np.float32)
    out[1:, :HY_EMB_DIM] = z[1:][::-1]
    out[:, hid:hid + HY_EMB_DIM] = z
    return out


def _hyena_filters(length, w1, b1, w2, b2, w3, b3, w4, freq):
    p = _conv_block(length)
    nb = length // p
    width = w4.shape[-1] // 4
    cb = min(FILTER_LANES, width)
    hid = w2.shape[0]
    z_all = jnp.asarray(_position_features(length, hid))
    both = lambda w: jnp.kron(jnp.eye(2, dtype=F32), w)
    twice = lambda v: jnp.tile(v.reshape(-1, hid), (1, 2))
    w1p = both(jnp.pad(w1, ((0, hid - HY_EMB_DIM), (0, 0))))
    w2, w3 = both(w2), both(w3)
    b1, b2, b3, freq = twice(b1), twice(b2), twice(b3), twice(freq)
    order = w4.shape[-1] // (2 * width)
    w4r = jnp.transpose(w4.reshape(w4.shape[0], order, 2, width), (1, 2, 0, 3))
    zero = jnp.zeros_like(w4r[:, 0])
    w4r = jnp.stack([jnp.concatenate([w4r[:, 1], zero], axis=1),
                     jnp.concatenate([zero, w4r[:, 0]], axis=1)], axis=1)
    hid = 2 * hid
    max_decay = math.log(HY_TARGET) / HY_FAST_DECAY
    min_decay = math.log(HY_TARGET) / HY_SLOW_DECAY
    deltas = jnp.abs(jnp.linspace(min_decay, max_decay, width, dtype=F32)).reshape(1, width)
    fwd = jnp.asarray(_dft_matrices(p)[0]).astype(BF16)
    full = lambda shape: pl.BlockSpec(shape, lambda o, c, di: (0,) * len(shape))
    return pl.pallas_call(
        functools.partial(_filter_kernel, length, p),
        grid=(order, width // cb, 2 * nb - 1),
        in_specs=[
            full(z_all.shape), full(w1p.shape), full((1, hid)), full(w2.shape), full((1, hid)),
            full(w3.shape), full((1, hid)), full(freq.shape),
            pl.BlockSpec((1, 2, hid, cb), lambda o, c, di: (o, 0, 0, c)),
            pl.BlockSpec((1, cb), lambda o, c, di: (0, c)),
            full(fwd.shape),
        ],
        out_specs=pl.BlockSpec((1, 1, 2 * p, cb), lambda o, c, di: (o, di, 0, c)),
        out_shape=jax.ShapeDtypeStruct((order, 2 * nb - 1, 2 * p, width), BF16),
        scratch_shapes=[pltpu.VMEM((length, hid), F32), pltpu.VMEM((2 * length, cb), F32),
                        pltpu.VMEM((2 * p, cb), F32), pltpu.VMEM((SUBLANES, cb), F32)],
        compiler_params=_cparams(("arbitrary", "arbitrary", "arbitrary")),
        name="hyena_filters",
    )(z_all, w1p, b1, w2, b2, w3, b3, freq, w4r, deltas, fwd)


def _short_conv_block(pad_ref, j, p, w_ref, b_ref):
    base = SUBLANES + j * p
    return (pad_ref[base - 1: base - 1 + p, :] * w_ref[0:1, :] + pad_ref[base: base + p, :] * w_ref[1:2, :]
            + pad_ref[base + 1: base + 1 + p, :] * w_ref[2:3, :] + b_ref[...])


def _zero_halo(pad_ref, length):
    zeros = jnp.zeros((SUBLANES, pad_ref.shape[1]), F32)
    pad_ref[0:SUBLANES, :] = zeros
    pad_ref[SUBLANES + length: 2 * SUBLANES + length, :] = zeros


def _fill_rows(pad_ref, src_ref, start, stop):
    pad_ref[SUBLANES + start: SUBLANES + stop, :] = src_ref[start:stop, :].astype(F32)


def _conv_kernel(length, p, conv_u, u_ref, m_ref, h_ref, f_ref, g_ref, cwu_ref, cbu_ref, cwm_ref, cbm_ref,
                 skip_ref, o_ref, pad, uconv, uhat, yhat):
    nb = length // p
    nbat = u_ref.shape[0]
    lanes = lambda bi: slice(bi * LANES, (bi + 1) * LANES)

    for bi in range(nbat):
        _zero_halo(pad.at[bi], length)
        if conv_u:
            _fill_rows(pad.at[bi], u_ref.at[bi], 0, p)
    fwd = f_ref[...]
    for j in range(nb):
        blocks = []
        for bi in range(nbat):
            if conv_u:
                if j + 1 < nb:
                    _fill_rows(pad.at[bi], u_ref.at[bi], (j + 1) * p, (j + 2) * p)
                uj = _short_conv_block(pad.at[bi], j, p, cwu_ref, cbu_ref)
                uconv[bi, j * p:(j + 1) * p, :] = uj
            else:
                uj = u_ref[bi, j * p:(j + 1) * p, :]
            blocks.append(uj.astype(BF16))
        uhat[j] = _dot(fwd, jnp.concatenate(blocks, axis=1))
    for bi in range(nbat):
        _fill_rows(pad.at[bi], m_ref.at[bi], 0, p)

    inv = g_ref[...]
    first = lax.broadcasted_iota(jnp.int32, (SUBLANES, nbat * LANES), 0) == 0
    for i in range(nb):
        for c in range(p // ACC_ROWS):
            ra = slice(c * ACC_ROWS, (c + 1) * ACC_ROWS)
            rb = slice(p + c * ACC_ROWS, p + (c + 1) * ACC_ROWS)
            acc = [[jnp.zeros((ACC_ROWS, LANES), F32) for _ in range(2)] for _ in range(nbat)]
            for j in range(nb):
                h_d = h_ref.at[0, i - j + nb - 1]
                ha, hb = h_d[ra, :].astype(F32), h_d[rb, :].astype(F32)
                for bi in range(nbat):
                    ua, ub = uhat[j, ra, lanes(bi)], uhat[j, rb, lanes(bi)]
                    acc[bi][0] = acc[bi][0] + (ha * ua - hb * ub)
                    acc[bi][1] = acc[bi][1] + (ha * ub + hb * ua)
            for bi in range(nbat):
                yhat[ra, lanes(bi)] = acc[bi][0]
                yhat[rb, lanes(bi)] = acc[bi][1]
        dc = jnp.zeros((SUBLANES, nbat * LANES), F32)
        ny = jnp.zeros((SUBLANES, nbat * LANES), F32)
        for j in range(nb):
            h_d = h_ref.at[0, i - j + nb - 1]
            h_dc = h_d[0:2 * SUBLANES, :].astype(F32)[0:SUBLANES]
            h_ny = h_d[p:p + 2 * SUBLANES, :].astype(F32)[0:SUBLANES]
            dc = dc + jnp.concatenate([h_dc] * nbat, axis=1) * uhat[j, 0:SUBLANES, :]
            ny = ny + jnp.concatenate([h_ny] * nbat, axis=1) * uhat[j, p:p + SUBLANES, :]
        yhat[0:SUBLANES, :] = jnp.where(first, dc, yhat[0:SUBLANES, :])
        yhat[p:p + SUBLANES, :] = jnp.where(first, ny, yhat[p:p + SUBLANES, :])

        y = _dot(inv, yhat[...].astype(BF16))
        for bi in range(nbat):
            if i + 1 < nb:
                _fill_rows(pad.at[bi], m_ref.at[bi], (i + 1) * p, (i + 2) * p)
            mi = _short_conv_block(pad.at[bi], i, p, cwm_ref, cbm_ref)
            src = uconv if conv_u else u_ref
            yi = y[:, lanes(bi)] + src[bi, i * p:(i + 1) * p, :].astype(F32) * skip_ref[...]
            o_ref[bi, i * p:(i + 1) * p, :] = (mi * yi).astype(o_ref.dtype)


def _long_conv_gate(u_src, u_col, conv_u, proj, m_col, spectra, order, conv_w, conv_b, skip, out_dtype):
    b, length, _ = proj.shape
    width = skip.shape[-1]
    p = _conv_block(length)
    nb = length // p
    cb = LANES
    ncb = width // cb
    nbat = CONV_BATCH if nb > 1 else b
    assert b % nbat == 0 and p % ACC_ROWS == 0
    fwd_np, inv_np = _dft_matrices(p)
    fwd = jnp.asarray(fwd_np).astype(BF16)
    inv = jnp.asarray(inv_np).astype(BF16)
    col = lambda base: (lambda c, bi: (bi, 0, base * ncb + c))
    par = lambda base: (lambda c, bi: (0, base * ncb + c))
    once = pl.Buffered(1)
    return pl.pallas_call(
        functools.partial(_conv_kernel, length, p, conv_u),
        grid=(ncb, b // nbat),
        in_specs=[
            pl.BlockSpec((nbat, length, cb), col(u_col)),
            pl.BlockSpec((nbat, length, cb), col(m_col)),
            pl.BlockSpec((1, 2 * nb - 1, 2 * p, cb), lambda c, bi: (order, 0, 0, c)),
            pl.BlockSpec(fwd.shape, lambda c, bi: (0, 0), pipeline_mode=once),
            pl.BlockSpec(inv.shape, lambda c, bi: (0, 0), pipeline_mode=once),
            pl.BlockSpec((3, cb), par(0)),
            pl.BlockSpec((1, cb), par(0)),
            pl.BlockSpec((3, cb), par(m_col)),
            pl.BlockSpec((1, cb), par(m_col)),
            pl.BlockSpec((1, cb), lambda c, bi: (0, c)),
        ],
        out_specs=pl.BlockSpec((nbat, length, cb), lambda c, bi: (bi, 0, c)),
        out_shape=jax.ShapeDtypeStruct((b, length, width), out_dtype),
        scratch_shapes=[
            pltpu.VMEM((nbat, length + 2 * SUBLANES, cb), F32),
            pltpu.VMEM((nbat, length, cb) if conv_u else (nbat, SUBLANES, cb), F32),
            pltpu.VMEM((nb, 2 * p, nbat * cb), F32), pltpu.VMEM((2 * p, nbat * cb), F32),
        ],
        compiler_params=_cparams(("arbitrary", "arbitrary")),
        name="long_conv_gate",
    )(u_src, proj, spectra, fwd, inv, conv_w, conv_b.reshape(1, -1), conv_w, conv_b.reshape(1, -1),
      skip[order].reshape(1, width))


def _hyena_core(proj, spectra, conv_w, conv_b, skip):
    z = _long_conv_gate(proj, 0, True, proj, 1, spectra, 0, conv_w, conv_b, skip, F32)
    return _long_conv_gate(z, 0, False, proj, 2, spectra, 1, conv_w, conv_b, skip, BF16)


def kernel(x, c, ctx, c_ctx, w_ada, b_ada, w_in, w_out, ln_g, ln_b, na_rpb, hy_conv_w, hy_conv_b, hy_f_w1, hy_f_b1,
           hy_f_w2, hy_f_b2, hy_f_w3, hy_f_b3, hy_f_w4, hy_f_freq, hy_skip):
    depth = w_in.shape[0]
    b, s, d = x.shape
    n_ctx = ctx.shape[1]
    n_mixers = 2
    alpha = (2 * depth) ** 0.25

    cc = jnp.concatenate([c, c_ctx[None, :], jnp.zeros((2 * SUBLANES - b - 1, d), F32)], axis=0)
    mod = _ada_vectors(cc, w_ada, b_ada)
    mod = mod.reshape(depth, mod.shape[1], 1, 3 * d)
    ln_g = ln_g.reshape(depth, 1, d)
    ln_b = ln_b.reshape(depth, 1, d)
    mask_table = jnp.asarray(_na_mask_table(s // GRID_W))

    for i in range(depth):
        last = i == depth - 1
        j = i // n_mixers
        attn = i % n_mixers == 0
        proj_dtype = F32 if attn else BF16
        proj, z = _in_projection(x, mod, i, None, w_in, proj_dtype, split_gate=attn)
        proj_c = z_c = None
        if attn or not last:
            proj_c, z_c = _in_projection(ctx, mod, i, b, w_in, proj_dtype, split_gate=attn)
        if attn:
            bias_table = _na_bias_table(na_rpb[j])
            a = _neighbourhood_attention(proj, proj_c, bias_table, mask_table)
            a_c = None if last else _context_attention(proj_c)
        else:
            filt = (hy_f_w1[j], hy_f_b1[j], hy_f_w2[j], hy_f_b2[j], hy_f_w3[j], hy_f_b3[j], hy_f_w4[j], hy_f_freq[j])
            a = _hyena_core(proj, _hyena_filters(s, *filt), hy_conv_w[j], hy_conv_b[j], hy_skip[j])
            a_c = None if last else _hyena_core(proj_c, _hyena_filters(n_ctx, *filt), hy_conv_w[j], hy_conv_b[j],
                                                hy_skip[j])
        x = _out_projection(a, z if attn else proj, x, mod, i, None, w_out, ln_g, ln_b, alpha)
        if not last:
            ctx = _out_projection(a_c, z_c if attn else proj_c, ctx, mod, i, b, w_out, ln_g, ln_b, alpha)
    return x
```

```python
import functools
import math

import numpy as np
import jax
import jax.numpy as jnp
from jax import lax
from jax.experimental import pallas as pl
from jax.experimental.pallas import tpu as pltpu

F32 = jnp.float32
BF16 = jnp.bfloat16

HEAD_DIM = 64
GRID_W = 64
KH = 8
KW = 16
Q_BLOCK_W = 16
K_BLOCK_W = Q_BLOCK_W + KW
N_COL_BLOCKS = GRID_W // Q_BLOCK_W
RPB_ROWS = 2 * KH - 1
RPB_COLS = 2 * KW - 1
HY_EMB_DIM = 33
HY_FAST_DECAY = 0.3
HY_SLOW_DECAY = 1.5
HY_TARGET = 1e-2
LN_EPS = 1e-5
NEG_INF = -1e30
LOG2E = 1.4426950408889634

LANES = 128
SUBLANES = 8
HEADS_PER_TILE = LANES // HEAD_DIM
VMEM_LIMIT = 56 * 1024 * 1024

Q_ROWS = 8
NA_BLOCKS_PER_STEP = 8
OUT_PROJ_ROWS = 1024
K_ROWS = 16
ROWS_PER_LANE_TILE = LANES // K_BLOCK_W
N_DSTART = 28
DSTART_SHIFT = 8
CONV_BLOCK = 1024
CONV_BATCH = 2
ACC_ROWS = 32
FILTER_LANES = 256
IN_PROJ_ROWS = 512


def _cparams(sem):
    return pltpu.CompilerParams(dimension_semantics=sem, vmem_limit_bytes=VMEM_LIMIT)


def _split_bf16(a):
    hi = a.astype(BF16)
    lo = (a - hi.astype(F32)).astype(BF16)
    return hi, lo


def _dot(a, b):
    return jnp.dot(a, b, preferred_element_type=F32)


def _dot_nt(a, b):
    return lax.dot_general(a, b, (((1,), (1,)), ((), ())), preferred_element_type=F32)


def _dot3(a, b):
    ah, al = _split_bf16(a)
    bh, bl = _split_bf16(b)
    return _dot(ah, bh) + _dot(al, bh) + _dot(ah, bl)


def _silu(x):
    return x * jax.nn.sigmoid(x)


def _ada_kernel(cc_ref, w_ref, b_ref, o_ref):
    o_ref[0] = _dot3(_silu(cc_ref[...]), w_ref[0]) + b_ref[0]


def _ada_vectors(cc, w_ada, b_ada):
    depth, d, n = w_ada.shape
    tn = min(n, 1024)
    return pl.pallas_call(
        _ada_kernel,
        grid=(depth, n // tn),
        in_specs=[
            pl.BlockSpec(cc.shape, lambda i, j: (0, 0)),
            pl.BlockSpec((1, d, tn), lambda i, j: (i, 0, j)),
            pl.BlockSpec((1, 1, tn), lambda i, j: (i, 0, j)),
        ],
        out_specs=pl.BlockSpec((1, cc.shape[0], tn), lambda i, j: (i, 0, j)),
        out_shape=jax.ShapeDtypeStruct((depth, cc.shape[0], n), F32),
        compiler_params=_cparams(("arbitrary", "arbitrary")),
        name="ada_vectors",
    )(cc, w_ada, b_ada.reshape(depth, 1, n))


def _mod_spec(d, layer, mod_row, part):
    return pl.BlockSpec((1, 1, 1, d), lambda bi, i: (layer, bi if mod_row is None else mod_row, 0, part))


def _first_step():
    return (pl.program_id(0) == 0) & (pl.program_id(1) == 0)


def _inproj_kernel(x_ref, sh_ref, sc_ref, w_ref, o_ref, gate_ref, w_bf):
    @pl.when(_first_step())
    def _():
        w_bf[...] = w_ref[0].astype(BF16)

    h = (x_ref[0] * (1.0 + sc_ref[0, 0]) + sh_ref[0, 0]).astype(BF16)
    n_main = o_ref.shape[-1]
    o_ref[0] = _dot(h, w_bf[:, 0:n_main]).astype(o_ref.dtype)
    if gate_ref:
        gate_ref[0][0] = _dot(h, w_bf[:, n_main:]).astype(gate_ref[0].dtype)


def _in_projection(x, mod, layer, mod_row, w, out_dtype, split_gate=False):
    b, s, d = x.shape
    n = w.shape[2]
    n_main = n - d if split_gate else n
    tm = min(s, IN_PROJ_ROWS * (F32.dtype.itemsize // jnp.dtype(out_dtype).itemsize))
    out_specs = [pl.BlockSpec((1, tm, n_main), lambda bi, i: (bi, i, 0))]
    out_shape = [jax.ShapeDtypeStruct((b, s, n_main), out_dtype)]
    if split_gate:
        out_specs.append(pl.BlockSpec((1, tm, d), lambda bi, i: (bi, i, 0)))
        out_shape.append(jax.ShapeDtypeStruct((b, s, d), BF16))

    def body(x_ref, sh_ref, sc_ref, w_ref, o_ref, *rest):
        _inproj_kernel(x_ref, sh_ref, sc_ref, w_ref, o_ref, rest[:-1], rest[-1])

    out = pl.pallas_call(
        body,
        grid=(b, s // tm),
        in_specs=[
            pl.BlockSpec((1, tm, d), lambda bi, i: (bi, i, 0)),
            _mod_spec(d, layer, mod_row, 0),
            _mod_spec(d, layer, mod_row, 1),
            pl.BlockSpec((1, d, n), lambda bi, i: (layer, 0, 0), pipeline_mode=pl.Buffered(1)),
        ],
        out_specs=out_specs,
        out_shape=out_shape,
        scratch_shapes=[pltpu.VMEM((d, n), BF16)],
        compiler_params=_cparams(("arbitrary", "arbitrary")),
        name="in_projection",
    )(x, mod, mod, w)
    return (out[0], out[1]) if split_gate else (out[0], None)


def _outproj_kernel(alpha, a_ref, z_ref, x_ref, gate_ref, w_ref, g_ref, b_ref, o_ref, w_bf):
    @pl.when(_first_step())
    def _():
        w_bf[...] = w_ref[0].astype(BF16)

    a = a_ref[0].astype(F32) * _silu(z_ref[0].astype(F32))
    y = _dot(a.astype(BF16), w_bf[...])
    r = x_ref[0] + (gate_ref[0, 0] * (1.0 / alpha)) * y
    mu = jnp.mean(r, axis=-1, keepdims=True)
    dlt = r - mu
    var = jnp.mean(dlt * dlt, axis=-1, keepdims=True)
    o_ref[0] = dlt * lax.rsqrt(var + LN_EPS / (alpha * alpha)) * g_ref[0] + b_ref[0]


def _out_projection(a, z_src, x, mod, layer, mod_row, w, ln_g, ln_b, alpha):
    b, s, d = x.shape
    z_col = z_src.shape[-1] // d - 1
    tm = min(s, OUT_PROJ_ROWS)
    row = lambda bi, i: (bi, i, 0)
    per_layer = lambda bi, i: (layer, 0, 0)
    return pl.pallas_call(
        functools.partial(_outproj_kernel, alpha),
        grid=(b, s // tm),
        in_specs=[
            pl.BlockSpec((1, tm, d), row),
            pl.BlockSpec((1, tm, d), lambda bi, i: (bi, i, z_col)),
            pl.BlockSpec((1, tm, d), row),
            _mod_spec(d, layer, mod_row, 2),
            pl.BlockSpec((1, d, d), per_layer),
            pl.BlockSpec((1, 1, d), per_layer),
            pl.BlockSpec((1, 1, d), per_layer),
        ],
        out_specs=pl.BlockSpec((1, tm, d), row),
        out_shape=jax.ShapeDtypeStruct((b, s, d), F32),
        scratch_shapes=[pltpu.VMEM((d, d), BF16)],
        compiler_params=_cparams(("arbitrary", "arbitrary")),
        name="out_projection",
    )(a, z_src, x, mod, w, ln_g, ln_b)


def _col_layout():
    q_cols = np.arange(GRID_W).reshape(N_COL_BLOCKS, Q_BLOCK_W)
    q_start = np.clip(q_cols - KW // 2, 0, GRID_W - KW)
    blk_start = np.clip(np.arange(N_COL_BLOCKS) * Q_BLOCK_W - KW // 2, 0, GRID_W - K_BLOCK_W)
    k_cols = blk_start[:, None] + np.arange(K_BLOCK_W)
    kc = k_cols[:, None, :]
    in_win = (kc >= q_start[:, :, None]) & (kc < q_start[:, :, None] + KW)
    dcol = np.clip(kc - q_cols[:, :, None] + KW - 1, 0, RPB_COLS - 1)
    return blk_start, in_win, dcol


def _window_base(rb, rows):
    return np.clip(rb * Q_ROWS - KH // 2, 0, rows - K_ROWS)


def _na_mask_table(rows):
    _, in_win, _ = _col_layout()
    n_rb = rows // Q_ROWS
    out = np.zeros((3, N_COL_BLOCKS, Q_ROWS, K_ROWS // ROWS_PER_LANE_TILE, Q_BLOCK_W, LANES), np.float32)
    for cls, rb in enumerate((0, 1, n_rb - 1)):
        kr0 = _window_base(rb, rows)
        for qr in range(Q_ROWS):
            r = rb * Q_ROWS + qr
            r0 = np.clip(r - KH // 2, 0, rows - KH)
            for kr in range(K_ROWS):
                row_ok = r0 <= kr0 + kr < r0 + KH
                g, j = divmod(kr, ROWS_PER_LANE_TILE)
                ok = in_win & row_ok
                out[cls, :, qr, g, :, j * K_BLOCK_W:(j + 1) * K_BLOCK_W] = np.where(ok, 0.0, NEG_INF)
    return out


def _rpb_expand_kernel(r_ref, e_ref, o_ref):
    r = r_ref[...]
    hi = r.astype(BF16)
    r1 = r - hi.astype(F32)
    mid = r1.astype(BF16)
    lo = (r1 - mid.astype(F32)).astype(BF16)
    e = e_ref[0]
    o_ref[0] = (_dot(hi, e) + _dot(mid, e) + _dot(lo, e)) * LOG2E


def _na_bias_table(rpb):
    h = rpb.shape[0]
    _, _, dcol = _col_layout()
    kpad = 32
    nj = ROWS_PER_LANE_TILE
    sel = dcol[:, None, :, :] == np.arange(kpad)[None, :, None, None]
    onehot = np.zeros((N_COL_BLOCKS, nj, kpad, Q_BLOCK_W, nj, K_BLOCK_W), np.float32)
    for j in range(nj):
        onehot[:, j, :, :, j, :] = sel
    onehot = onehot.reshape(N_COL_BLOCKS, nj * kpad, Q_BLOCK_W * LANES)
    hi_pad = N_DSTART + nj - 1 - DSTART_SHIFT - RPB_ROWS
    rp = jnp.pad(rpb, ((0, 0), (DSTART_SHIFT, hi_pad), (0, kpad - RPB_COLS)))
    shifted = jnp.concatenate([rp[:, j:j + N_DSTART] for j in range(nj)], axis=-1).reshape(h * N_DSTART, nj * kpad)
    t = pl.pallas_call(
        _rpb_expand_kernel,
        grid=(N_COL_BLOCKS,),
        in_specs=[
            pl.BlockSpec(shifted.shape, lambda n: (0, 0)),
            pl.BlockSpec((1, nj * kpad, Q_BLOCK_W * LANES), lambda n: (n, 0, 0)),
        ],
        out_specs=pl.BlockSpec((1, h * N_DSTART, Q_BLOCK_W * LANES), lambda n: (n, 0, 0)),
        out_shape=jax.ShapeDtypeStruct((N_COL_BLOCKS, h * N_DSTART, Q_BLOCK_W * LANES), F32),
        compiler_params=_cparams(("arbitrary",)),
        name="rpb_expand",
    )(shifted, jnp.asarray(onehot, BF16))
    return t.reshape(N_COL_BLOCKS, h // HEADS_PER_TILE, HEADS_PER_TILE, N_DSTART, Q_BLOCK_W, LANES)


def _na_kernel(rows, blocks_per_step, blk_start, q_ref, k_ref, v_ref, kc_ref, vc_ref, t_ref, m_ref, o_ref):
    n_rb = rows // Q_ROWS
    lane = lax.broadcasted_iota(jnp.int32, (1, LANES), 1)
    kctx = kc_ref[0].astype(BF16)
    vctx = vc_ref[0].astype(BF16)
    n_groups = K_ROWS // ROWS_PER_LANE_TILE
    nq = Q_ROWS * Q_BLOCK_W
    rows_n = HEADS_PER_TILE * nq

    def prepare(sub):
        rb = pl.program_id(2) * blocks_per_step + sub
        kr0 = jnp.clip(rb * Q_ROWS - KH // 2, 0, rows - K_ROWS)
        cls = jnp.where(rb == 0, 0, jnp.where(rb == n_rb - 1, 2, 1))
        off = kr0 - rb * Q_ROWS + KH - 1 + DSTART_SHIFT
        q0 = sub * Q_ROWS * GRID_W
        q2s = []
        for n in range(N_COL_BLOCKS):
            qn = jnp.concatenate(
                [q_ref[0, q0 + qr * GRID_W + n * Q_BLOCK_W: q0 + qr * GRID_W + (n + 1) * Q_BLOCK_W, :]
                 for qr in range(Q_ROWS)], axis=0) * (HEAD_DIM ** -0.5 * LOG2E)
            q2s.append(jnp.concatenate(
                [jnp.where((lane >= h * HEAD_DIM) & (lane < (h + 1) * HEAD_DIM), qn, 0.0)
                 for h in range(HEADS_PER_TILE)], axis=0).astype(BF16))
        s_ctx_all = _dot_nt(jnp.concatenate(q2s, axis=0), kctx)
        return kr0, cls, off, q0, q2s, s_ctx_all

    def scores(prep, n):
        kr0, cls, off, _, q2s, s_ctx_all = prep
        c0 = int(blk_start[n])
        window = lambda ref: jnp.concatenate(
            [ref[0, pl.ds(pl.multiple_of((kr0 + kr) * GRID_W + c0, SUBLANES), K_BLOCK_W), :] for kr in range(K_ROWS)],
            axis=0).astype(BF16)
        bias = jnp.concatenate([
            jnp.concatenate([t_ref[n, 0, h, off + ROWS_PER_LANE_TILE * g - qr] + m_ref[cls, n, qr, g]
                             for g in range(n_groups)], axis=1)
            for h in range(HEADS_PER_TILE) for qr in range(Q_ROWS)], axis=0)
        return _dot_nt(q2s[n], window(k_ref)) + bias, s_ctx_all[n * rows_n:(n + 1) * rows_n], window(v_ref)

    def finish(q0, n, s_lat, s_ctx, vn):
        m = jnp.maximum(jnp.max(s_lat, axis=-1, keepdims=True), jnp.max(s_ctx, axis=-1, keepdims=True))
        e_lat = jnp.exp2(s_lat - m)
        e_ctx = jnp.exp2(s_ctx - m)
        den = jnp.sum(e_lat, axis=-1, keepdims=True) + jnp.sum(e_ctx, axis=-1, keepdims=True)
        o2 = (_dot(e_lat.astype(BF16), vn) + _dot(e_ctx.astype(BF16), vctx)) / den
        o_n = o2[0:nq]
        for h in range(1, HEADS_PER_TILE):
            o_n = jnp.where((lane >= h * HEAD_DIM) & (lane < (h + 1) * HEAD_DIM), o2[h * nq:(h + 1) * nq], o_n)
        for qr in range(Q_ROWS):
            o_ref[0, q0 + qr * GRID_W + n * Q_BLOCK_W: q0 + qr * GRID_W + (n + 1) * Q_BLOCK_W, :] = (
                o_n[qr * Q_BLOCK_W:(qr + 1) * Q_BLOCK_W].astype(o_ref.dtype))

    order = [(sub, n) for sub in range(blocks_per_step) for n in range(N_COL_BLOCKS)]
    preps = {0: prepare(0)}
    cur = scores(preps[0], 0)
    for idx, (sub, n) in enumerate(order):
        nxt = None
        if idx + 1 < len(order):
            nsub, nn = order[idx + 1]
            if nsub not in preps:
                preps[nsub] = prepare(nsub)
            nxt = scores(preps[nsub], nn)
        finish(preps[sub][3], n, *cur)
        cur = nxt


def _neighbourhood_attention(proj, proj_c, bias_table, mask_table):
    b, s, d3 = proj.shape
    d = d3 // 3
    c = proj_c.shape[1]
    rows = s // GRID_W
    n_tiles = d // LANES
    blk_start, _, _ = _col_layout()
    blocks_per_step = min(NA_BLOCKS_PER_STEP, rows // Q_ROWS)
    tq = blocks_per_step * Q_ROWS * GRID_W
    return pl.pallas_call(
        functools.partial(_na_kernel, rows, blocks_per_step, blk_start),
        grid=(b, n_tiles, s // tq),
        in_specs=[
            pl.BlockSpec((1, tq, LANES), lambda bi, hp, rb: (bi, rb, hp)),
            pl.BlockSpec((1, s, LANES), lambda bi, hp, rb: (bi, 0, n_tiles + hp)),
            pl.BlockSpec((1, s, LANES), lambda bi, hp, rb: (bi, 0, 2 * n_tiles + hp)),
            pl.BlockSpec((1, c, LANES), lambda bi, hp, rb: (bi, 0, n_tiles + hp)),
            pl.BlockSpec((1, c, LANES), lambda bi, hp, rb: (bi, 0, 2 * n_tiles + hp)),
            pl.BlockSpec(bias_table.shape[:1] + (1,) + bias_table.shape[2:], lambda bi, hp, rb: (0, hp, 0, 0, 0, 0)),
            pl.BlockSpec(mask_table.shape, lambda bi, hp, rb: (0, 0, 0, 0, 0, 0)),
        ],
        out_specs=pl.BlockSpec((1, tq, LANES), lambda bi, hp, rb: (bi, rb, hp)),
        out_shape=jax.ShapeDtypeStruct((b, s, d), BF16),
        compiler_params=_cparams(("arbitrary", "arbitrary", "arbitrary")),
        name="neighbourhood_attention",
    )(proj, proj, proj, proj_c, proj_c, bias_table, mask_table)


def _ctx_attn_kernel(q_ref, k_ref, v_ref, o_ref):
    lane = lax.broadcasted_iota(jnp.int32, (1, LANES), 1)
    for t in range(q_ref.shape[-1] // LANES):
        cols = slice(t * LANES, (t + 1) * LANES)
        q = q_ref[0, :, cols] * (HEAD_DIM ** -0.5 * LOG2E)
        k = k_ref[0, :, cols].astype(BF16)
        v = v_ref[0, :, cols].astype(BF16)
        out = None
        for h in range(HEADS_PER_TILE):
            in_head = (lane >= h * HEAD_DIM) & (lane < (h + 1) * HEAD_DIM)
            s = _dot_nt(jnp.where(in_head, q, 0.0).astype(BF16), k)
            e = jnp.exp2(s - jnp.max(s, axis=-1, keepdims=True))
            o_h = _dot(e.astype(BF16), v) / jnp.sum(e, axis=-1, keepdims=True)
            out = o_h if out is None else jnp.where(in_head, o_h, out)
        o_ref[0, :, cols] = out.astype(o_ref.dtype)


def _context_attention(proj_c):
    b, c, d3 = proj_c.shape
    d = d3 // 3
    return pl.pallas_call(
        _ctx_attn_kernel,
        grid=(b,),
        in_specs=[
            pl.BlockSpec((1, c, d), lambda bi: (bi, 0, 0)),
            pl.BlockSpec((1, c, d), lambda bi: (bi, 0, 1)),
            pl.BlockSpec((1, c, d), lambda bi: (bi, 0, 2)),
        ],
        out_specs=pl.BlockSpec((1, c, d), lambda bi: (bi, 0, 0)),
        out_shape=jax.ShapeDtypeStruct((b, c, d), BF16),
        compiler_params=_cparams(("arbitrary",)),
        name="context_attention",
    )(proj_c, proj_c, proj_c)


def _conv_block(length):
    return min(CONV_BLOCK, length)


def _dft_matrices(p):
    k = np.arange(p, dtype=np.float64)[:, None]
    n = np.arange(p, dtype=np.float64)[None, :]
    ang = 2.0 * np.pi * k * n / (2 * p)
    f_im = -np.sin(ang)
    f_im[0] = np.cos(np.pi * n[0])
    fwd = np.concatenate([np.cos(ang), f_im], axis=0)
    wk = np.where(k == 0, 1.0, 2.0) / (2 * p)
    ang_i = 2.0 * np.pi * k * (n + p) / (2 * p)
    g_im = -np.sin(ang_i) * wk
    g_im[0] = np.cos(np.pi * (n[0] + p)) / (2 * p)
    inv = np.concatenate([np.cos(ang_i) * wk, g_im], axis=0).T
    return fwd.astype(np.float32), inv.astype(np.float32)


def _filter_kernel(length, p, z_ref, w1_ref, b1_ref, w2_ref, b2_ref, w3_ref, b3_ref, fr_ref, w4_ref,
                   dl_ref, f_ref, o_ref, a_scr, t_scr, prev_scr, norm_scr):
    nb = length // p
    di = pl.program_id(2)
    cb = o_ref.shape[-1]

    @pl.when((pl.program_id(0) == 0) & (pl.program_id(1) == 0) & (di == 0))
    def _():
        a = jnp.sin(fr_ref[0:1, :] * (_dot3(z_ref[...], w1_ref[...]) + b1_ref[...]))
        a = jnp.sin(fr_ref[1:2, :] * (_dot3(a, w2_ref[...]) + b2_ref[...]))
        a_scr[...] = jnp.sin(fr_ref[2:3, :] * (_dot3(a, w3_ref[...]) + b3_ref[...]))

    def block_dft(blk):
        start = pl.multiple_of(blk * p, p)
        return _dot(f_ref[...], t_scr[pl.ds(start, p), :].astype(BF16))

    @pl.when(di == 0)
    def _():
        total = jnp.zeros((1, cb), F32)
        for blk in range(2 * nb):
            back = blk < nb
            idx = lax.broadcasted_iota(jnp.int32, (p, cb), 0) + (blk % nb) * p
            t = ((length - idx) if back else idx).astype(F32) * (1.0 / (length - 1))
            rows_a = (blk % nb) * p
            taps = (_dot(a_scr[rows_a:rows_a + p, :].astype(BF16), w4_ref[0, 0 if back else 1].astype(BF16))
                    * jnp.exp(-t * dl_ref[...]))
            if blk == 0:
                taps = jnp.where(idx == 0, 0.0, taps)
            t_scr[blk * p:(blk + 1) * p, :] = taps
            total = total + jnp.sum(jnp.abs(taps), axis=0, keepdims=True)
        norm_scr[...] = jnp.broadcast_to(1.0 / total, norm_scr.shape)
        prev_scr[...] = block_dft(0)

    nxt = block_dft(di + 1)
    krow = lax.broadcasted_iota(jnp.int32, (2 * p, cb), 0)
    o_ref[0, 0] = ((prev_scr[...] + jnp.where((krow & 1) == 0, nxt, -nxt)) * norm_scr[0:1, :]).astype(o_ref.dtype)
    prev_scr[...] = nxt


def _position_features(length, hid):
    t = np.linspace(0.0, 1.0, length)[:, None]
    bands = (HY_EMB_DIM - 1) // 2
    wpos = 2.0 * np.pi * np.arange(length)[:, None] / length
    f = np.linspace(1e-4, bands - 1, bands)[None, :]
    z = np.concatenate([t, np.cos(f * wpos), -np.sin(f * wpos)], axis=-1)
    out = np.zeros((length, 2 * hid), np.float32)
    out[1:, :HY_EMB_DIM] = z[1:][::-1]
    out[:, hid:hid + HY_EMB_DIM] = z
    return out


def _hyena_filters(length, w1, b1, w2, b2, w3, b3, w4, freq):
    p = _conv_block(length)
    nb = length // p
    width = w4.shape[-1] // 4
    cb = min(FILTER_LANES, width)
    hid = w2.shape[0]
    z_all = jnp.asarray(_position_features(length, hid))
    both = lambda w: jnp.kron(jnp.eye(2, dtype=F32), w)
    twice = lambda v: jnp.tile(v.reshape(-1, hid), (1, 2))
    w1p = both(jnp.pad(w1, ((0, hid - HY_EMB_DIM), (0, 0))))
    w2, w3 = both(w2), both(w3)
    b1, b2, b3, freq = twice(b1), twice(b2), twice(b3), twice(freq)
    order = w4.shape[-1] // (2 * width)
    w4r = jnp.transpose(w4.reshape(w4.shape[0], order, 2, width), (1, 2, 0, 3))
    zero = jnp.zeros_like(w4r[:, 0])
    w4r = jnp.stack([jnp.concatenate([w4r[:, 1], zero], axis=1),
                     jnp.concatenate([zero, w4r[:, 0]], axis=1)], axis=1)
    hid = 2 * hid
    max_decay = math.log(HY_TARGET) / HY_FAST_DECAY
    min_decay = math.log(HY_TARGET) / HY_SLOW_DECAY
    deltas = jnp.abs(jnp.linspace(min_decay, max_decay, width, dtype=F32)).reshape(1, width)
    fwd = jnp.asarray(_dft_matrices(p)[0]).astype(BF16)
    full = lambda shape: pl.BlockSpec(shape, lambda o, c, di: (0,) * len(shape))
    return pl.pallas_call(
        functools.partial(_filter_kernel, length, p),
        grid=(order, width // cb, 2 * nb - 1),
        in_specs=[
            full(z_all.shape), full(w1p.shape), full((1, hid)), full(w2.shape), full((1, hid)),
            full(w3.shape), full((1, hid)), full(freq.shape),
            pl.BlockSpec((1, 2, hid, cb), lambda o, c, di: (o, 0, 0, c)),
            pl.BlockSpec((1, cb), lambda o, c, di: (0, c)),
            full(fwd.shape),
        ],
        out_specs=pl.BlockSpec((1, 1, 2 * p, cb), lambda o, c, di: (o, di, 0, c)),
        out_shape=jax.ShapeDtypeStruct((order, 2 * nb - 1, 2 * p, width), BF16),
        scratch_shapes=[pltpu.VMEM((length, hid), F32), pltpu.VMEM((2 * length, cb), F32),
                        pltpu.VMEM((2 * p, cb), F32), pltpu.VMEM((SUBLANES, cb), F32)],
        compiler_params=_cparams(("arbitrary", "arbitrary", "arbitrary")),
        name="hyena_filters",
    )(z_all, w1p, b1, w2, b2, w3, b3, freq, w4r, deltas, fwd)


def _short_conv_block(pad_ref, j, p, w_ref, b_ref):
    base = SUBLANES + j * p
    return (pad_ref[base - 1: base - 1 + p, :] * w_ref[0:1, :] + pad_ref[base: base + p, :] * w_ref[1:2, :]
            + pad_ref[base + 1: base + 1 + p, :] * w_ref[2:3, :] + b_ref[...])


def _zero_halo(pad_ref, length):
    zeros = jnp.zeros((SUBLANES, pad_ref.shape[1]), F32)
    pad_ref[0:SUBLANES, :] = zeros
    pad_ref[SUBLANES + length: 2 * SUBLANES + length, :] = zeros


def _fill_rows(pad_ref, src_ref, start, stop):
    pad_ref[SUBLANES + start: SUBLANES + stop, :] = src_ref[start:stop, :].astype(F32)


def _conv_kernel(length, p, conv_u, u_ref, m_ref, h_ref, f_ref, g_ref, cwu_ref, cbu_ref, cwm_ref, cbm_ref,
                 skip_ref, o_ref, pad, uconv, uhat, yhat):
    nb = length // p
    nbat = u_ref.shape[0]
    lanes = lambda bi: slice(bi * LANES, (bi + 1) * LANES)

    for bi in range(nbat):
        _zero_halo(pad.at[bi], length)
        if conv_u:
            _fill_rows(pad.at[bi], u_ref.at[bi], 0, p)
    fwd = f_ref[...]
    for j in range(nb):
        blocks = []
        for bi in range(nbat):
            if conv_u:
                if j + 1 < nb:
                    _fill_rows(pad.at[bi], u_ref.at[bi], (j + 1) * p, (j + 2) * p)
                uj = _short_conv_block(pad.at[bi], j, p, cwu_ref, cbu_ref)
                uconv[bi, j * p:(j + 1) * p, :] = uj
            else:
                uj = u_ref[bi, j * p:(j + 1) * p, :]
            blocks.append(uj.astype(BF16))
        uhat[j] = _dot(fwd, jnp.concatenate(blocks, axis=1))
    for bi in range(nbat):
        _fill_rows(pad.at[bi], m_ref.at[bi], 0, p)

    inv = g_ref[...]
    first = lax.broadcasted_iota(jnp.int32, (SUBLANES, nbat * LANES), 0) == 0
    for i in range(nb):
        for c in range(p // ACC_ROWS):
            ra = slice(c * ACC_ROWS, (c + 1) * ACC_ROWS)
            rb = slice(p + c * ACC_ROWS, p + (c + 1) * ACC_ROWS)
            acc = [[jnp.zeros((ACC_ROWS, LANES), F32) for _ in range(2)] for _ in range(nbat)]
            for j in range(nb):
                h_d = h_ref.at[0, i - j + nb - 1]
                ha, hb = h_d[ra, :].astype(F32), h_d[rb, :].astype(F32)
                for bi in range(nbat):
                    ua, ub = uhat[j, ra, lanes(bi)], uhat[j, rb, lanes(bi)]
                    acc[bi][0] = acc[bi][0] + (ha * ua - hb * ub)
                    acc[bi][1] = acc[bi][1] + (ha * ub + hb * ua)
            for bi in range(nbat):
                yhat[ra, lanes(bi)] = acc[bi][0]
                yhat[rb, lanes(bi)] = acc[bi][1]
        dc = jnp.zeros((SUBLANES, nbat * LANES), F32)
        ny = jnp.zeros((SUBLANES, nbat * LANES), F32)
        for j in range(nb):
            h_d = h_ref.at[0, i - j + nb - 1]
            h_dc = h_d[0:2 * SUBLANES, :].astype(F32)[0:SUBLANES]
            h_ny = h_d[p:p + 2 * SUBLANES, :].astype(F32)[0:SUBLANES]
            dc = dc + jnp.concatenate([h_dc] * nbat, axis=1) * uhat[j, 0:SUBLANES, :]
            ny = ny + jnp.concatenate([h_ny] * nbat, axis=1) * uhat[j, p:p + SUBLANES, :]
        yhat[0:SUBLANES, :] = jnp.where(first, dc, yhat[0:SUBLANES, :])
        yhat[p:p + SUBLANES, :] = jnp.where(first, ny, yhat[p:p + SUBLANES, :])

        y = _dot(inv, yhat[...].astype(BF16))
        for bi in range(nbat):
            if i + 1 < nb:
                _fill_rows(pad.at[bi], m_ref.at[bi], (i + 1) * p, (i + 2) * p)
            mi = _short_conv_block(pad.at[bi], i, p, cwm_ref, cbm_ref)
            src = uconv if conv_u else u_ref
            yi = y[:, lanes(bi)] + src[bi, i * p:(i + 1) * p, :].astype(F32) * skip_ref[...]
            o_ref[bi, i * p:(i + 1) * p, :] = (mi * yi).astype(o_ref.dtype)


def _long_conv_gate(u_src, u_col, conv_u, proj, m_col, spectra, order, conv_w, conv_b, skip, out_dtype):
    b, length, _ = proj.shape
    width = skip.shape[-1]
    p = _conv_block(length)
    nb = length // p
    cb = LANES
    ncb = width // cb
    nbat = CONV_BATCH if nb > 1 else b
    assert b % nbat == 0 and p % ACC_ROWS == 0
    fwd_np, inv_np = _dft_matrices(p)
    fwd = jnp.asarray(fwd_np).astype(BF16)
    inv = jnp.asarray(inv_np).astype(BF16)
    col = lambda base: (lambda c, bi: (bi, 0, base * ncb + c))
    par = lambda base: (lambda c, bi: (0, base * ncb + c))
    once = pl.Buffered(1)
    return pl.pallas_call(
        functools.partial(_conv_kernel, length, p, conv_u),
        grid=(ncb, b // nbat),
        in_specs=[
            pl.BlockSpec((nbat, length, cb), col(u_col)),
            pl.BlockSpec((nbat, length, cb), col(m_col)),
            pl.BlockSpec((1, 2 * nb - 1, 2 * p, cb), lambda c, bi: (order, 0, 0, c)),
            pl.BlockSpec(fwd.shape, lambda c, bi: (0, 0), pipeline_mode=once),
            pl.BlockSpec(inv.shape, lambda c, bi: (0, 0), pipeline_mode=once),
            pl.BlockSpec((3, cb), par(0)),
            pl.BlockSpec((1, cb), par(0)),
            pl.BlockSpec((3, cb), par(m_col)),
            pl.BlockSpec((1, cb), par(m_col)),
            pl.BlockSpec((1, cb), lambda c, bi: (0, c)),
        ],
        out_specs=pl.BlockSpec((nbat, length, cb), lambda c, bi: (bi, 0, c)),
        out_shape=jax.ShapeDtypeStruct((b, length, width), out_dtype),
        scratch_shapes=[
            pltpu.VMEM((nbat, length + 2 * SUBLANES, cb), F32),
            pltpu.VMEM((nbat, length, cb) if conv_u else (nbat, SUBLANES, cb), F32),
            pltpu.VMEM((nb, 2 * p, nbat * cb), F32), pltpu.VMEM((2 * p, nbat * cb), F32),
        ],
        compiler_params=_cparams(("arbitrary", "arbitrary")),
        name="long_conv_gate",
    )(u_src, proj, spectra, fwd, inv, conv_w, conv_b.reshape(1, -1), conv_w, conv_b.reshape(1, -1),
      skip[order].reshape(1, width))


def _hyena_core(proj, spectra, conv_w, conv_b, skip):
    z = _long_conv_gate(proj, 0, True, proj, 1, spectra, 0, conv_w, conv_b, skip, F32)
    return _long_conv_gate(z, 0, False, proj, 2, spectra, 1, conv_w, conv_b, skip, BF16)


def kernel(x, c, ctx, c_ctx, w_ada, b_ada, w_in, w_out, ln_g, ln_b, na_rpb, hy_conv_w, hy_conv_b, hy_f_w1, hy_f_b1,
           hy_f_w2, hy_f_b2, hy_f_w3, hy_f_b3, hy_f_w4, hy_f_freq, hy_skip):
    depth = w_in.shape[0]
    b, s, d = x.shape
    n_ctx = ctx.shape[1]
    n_mixers = 2
    alpha = (2 * depth) ** 0.25

    cc = jnp.concatenate([c, c_ctx[None, :], jnp.zeros((2 * SUBLANES - b - 1, d), F32)], axis=0)
    mod = _ada_vectors(cc, w_ada, b_ada)
    mod = mod.reshape(depth, mod.shape[1], 1, 3 * d)
    ln_g = ln_g.reshape(depth, 1, d)
    ln_b = ln_b.reshape(depth, 1, d)
    mask_table = jnp.asarray(_na_mask_table(s // GRID_W))

    for i in range(depth):
        last = i == depth - 1
        j = i // n_mixers
        attn = i % n_mixers == 0
        proj_dtype = F32 if attn else BF16
        proj, z = _in_projection(x, mod, i, None, w_in, proj_dtype, split_gate=attn)
        proj_c = z_c = None
        if attn or not last:
            proj_c, z_c = _in_projection(ctx, mod, i, b, w_in, proj_dtype, split_gate=attn)
        if attn:
            bias_table = _na_bias_table(na_rpb[j])
            a = _neighbourhood_attention(proj, proj_c, bias_table, mask_table)
            a_c = None if last else _context_attention(proj_c)
        else:
            filt = (hy_f_w1[j], hy_f_b1[j], hy_f_w2[j], hy_f_b2[j], hy_f_w3[j], hy_f_b3[j], hy_f_w4[j], hy_f_freq[j])
            a = _hyena_core(proj, _hyena_filters(s, *filt), hy_conv_w[j], hy_conv_b[j], hy_skip[j])
            a_c = None if last else _hyena_core(proj_c, _hyena_filters(n_ctx, *filt), hy_conv_w[j], hy_conv_b[j],
                                                hy_skip[j])
        x = _out_projection(a, z if attn else proj, x, mod, i, None, w_out, ln_g, ln_b, alpha)
        if not last:
            ctx = _out_projection(a_c, z_c if attn else proj_c, ctx, mod, i, b, w_out, ln_g, ln_b, alpha)
    return x
```

```python
import functools
import math

import numpy as np
import jax
import jax.numpy as jnp
from jax import lax
from jax.experimental import pallas as pl
from jax.experimental.pallas import tpu as pltpu

F32 = jnp.float32
BF16 = jnp.bfloat16

HEAD_DIM = 64
GRID_W = 64
KH = 8
KW = 16
Q_BLOCK_W = 16
K_BLOCK_W = Q_BLOCK_W + KW
N_COL_BLOCKS = GRID_W // Q_BLOCK_W
RPB_ROWS = 2 * KH - 1
RPB_COLS = 2 * KW - 1
HY_EMB_DIM = 33
HY_FAST_DECAY = 0.3
HY_SLOW_DECAY = 1.5
HY_TARGET = 1e-2
LN_EPS = 1e-5
NEG_INF = -1e30
LOG2E = 1.4426950408889634

LANES = 128
SUBLANES = 8
HEADS_PER_TILE = LANES // HEAD_DIM
VMEM_LIMIT = 56 * 1024 * 1024

Q_ROWS = 8
NA_BLOCKS_PER_STEP = 8
OUT_PROJ_ROWS = 1024
K_ROWS = 16
ROWS_PER_LANE_TILE = LANES // K_BLOCK_W
N_DSTART = 28
DSTART_SHIFT = 8
CONV_BLOCK = 1024
CONV_BATCH = 2
ACC_ROWS = 32
FILTER_LANES = 512
IN_PROJ_ROWS = 512


def _cparams(sem):
    return pltpu.CompilerParams(dimension_semantics=sem, vmem_limit_bytes=VMEM_LIMIT)


def _split_bf16(a):
    hi = a.astype(BF16)
    lo = (a - hi.astype(F32)).astype(BF16)
    return hi, lo


def _dot(a, b):
    return jnp.dot(a, b, preferred_element_type=F32)


def _dot_nt(a, b):
    return lax.dot_general(a, b, (((1,), (1,)), ((), ())), preferred_element_type=F32)


def _dot3(a, b):
    ah, al = _split_bf16(a)
    bh, bl = _split_bf16(b)
    return _dot(ah, bh) + _dot(al, bh) + _dot(ah, bl)


def _silu(x):
    return x * jax.nn.sigmoid(x)


def _ada_kernel(cc_ref, w_ref, b_ref, o_ref):
    o_ref[0] = _dot3(_silu(cc_ref[...]), w_ref[0]) + b_ref[0]


def _ada_vectors(cc, w_ada, b_ada):
    depth, d, n = w_ada.shape
    tn = min(n, 1024)
    return pl.pallas_call(
        _ada_kernel,
        grid=(depth, n // tn),
        in_specs=[
            pl.BlockSpec(cc.shape, lambda i, j: (0, 0)),
            pl.BlockSpec((1, d, tn), lambda i, j: (i, 0, j)),
            pl.BlockSpec((1, 1, tn), lambda i, j: (i, 0, j)),
        ],
        out_specs=pl.BlockSpec((1, cc.shape[0], tn), lambda i, j: (i, 0, j)),
        out_shape=jax.ShapeDtypeStruct((depth, cc.shape[0], n), F32),
        compiler_params=_cparams(("arbitrary", "arbitrary")),
        name="ada_vectors",
    )(cc, w_ada, b_ada.reshape(depth, 1, n))


def _mod_spec(d, layer, mod_row, part):
    return pl.BlockSpec((1, 1, 1, d), lambda bi, i: (layer, bi if mod_row is None else mod_row, 0, part))


def _first_step():
    return (pl.program_id(0) == 0) & (pl.program_id(1) == 0)


def _inproj_kernel(x_ref, sh_ref, sc_ref, w_ref, o_ref, gate_ref, w_bf):
    @pl.when(_first_step())
    def _():
        w_bf[...] = w_ref[0].astype(BF16)

    h = (x_ref[0] * (1.0 + sc_ref[0, 0]) + sh_ref[0, 0]).astype(BF16)
    n_main = o_ref.shape[-1]
    o_ref[0] = _dot(h, w_bf[:, 0:n_main]).astype(o_ref.dtype)
    if gate_ref:
        gate_ref[0][0] = _dot(h, w_bf[:, n_main:]).astype(gate_ref[0].dtype)


def _in_projection(x, mod, layer, mod_row, w, out_dtype, split_gate=False):
    b, s, d = x.shape
    n = w.shape[2]
    n_main = n - d if split_gate else n
    tm = min(s, IN_PROJ_ROWS * (F32.dtype.itemsize // jnp.dtype(out_dtype).itemsize))
    out_specs = [pl.BlockSpec((1, tm, n_main), lambda bi, i: (bi, i, 0))]
    out_shape = [jax.ShapeDtypeStruct((b, s, n_main), out_dtype)]
    if split_gate:
        out_specs.append(pl.BlockSpec((1, tm, d), lambda bi, i: (bi, i, 0)))
        out_shape.append(jax.ShapeDtypeStruct((b, s, d), BF16))

    def body(x_ref, sh_ref, sc_ref, w_ref, o_ref, *rest):
        _inproj_kernel(x_ref, sh_ref, sc_ref, w_ref, o_ref, rest[:-1], rest[-1])

    out = pl.pallas_call(
        body,
        grid=(b, s // tm),
        in_specs=[
            pl.BlockSpec((1, tm, d), lambda bi, i: (bi, i, 0)),
            _mod_spec(d, layer, mod_row, 0),
            _mod_spec(d, layer, mod_row, 1),
            pl.BlockSpec((1, d, n), lambda bi, i: (layer, 0, 0), pipeline_mode=pl.Buffered(1)),
        ],
        out_specs=out_specs,
        out_shape=out_shape,
        scratch_shapes=[pltpu.VMEM((d, n), BF16)],
        compiler_params=_cparams(("arbitrary", "arbitrary")),
        name="in_projection",
    )(x, mod, mod, w)
    return (out[0], out[1]) if split_gate else (out[0], None)


def _outproj_kernel(alpha, a_ref, z_ref, x_ref, gate_ref, w_ref, g_ref, b_ref, o_ref, w_bf):
    @pl.when(_first_step())
    def _():
        w_bf[...] = w_ref[0].astype(BF16)

    a = a_ref[0].astype(F32) * _silu(z_ref[0].astype(F32))
    y = _dot(a.astype(BF16), w_bf[...])
    r = x_ref[0] + (gate_ref[0, 0] * (1.0 / alpha)) * y
    mu = jnp.mean(r, axis=-1, keepdims=True)
    dlt = r - mu
    var = jnp.mean(dlt * dlt, axis=-1, keepdims=True)
    o_ref[0] = dlt * lax.rsqrt(var + LN_EPS / (alpha * alpha)) * g_ref[0] + b_ref[0]


def _out_projection(a, z_src, x, mod, layer, mod_row, w, ln_g, ln_b, alpha):
    b, s, d = x.shape
    z_col = z_src.shape[-1] // d - 1
    tm = min(s, OUT_PROJ_ROWS)
    row = lambda bi, i: (bi, i, 0)
    per_layer = lambda bi, i: (layer, 0, 0)
    return pl.pallas_call(
        functools.partial(_outproj_kernel, alpha),
        grid=(b, s // tm),
        in_specs=[
            pl.BlockSpec((1, tm, d), row),
            pl.BlockSpec((1, tm, d), lambda bi, i: (bi, i, z_col)),
            pl.BlockSpec((1, tm, d), row),
            _mod_spec(d, layer, mod_row, 2),
            pl.BlockSpec((1, d, d), per_layer),
            pl.BlockSpec((1, 1, d), per_layer),
            pl.BlockSpec((1, 1, d), per_layer),
        ],
        out_specs=pl.BlockSpec((1, tm, d), row),
        out_shape=jax.ShapeDtypeStruct((b, s, d), F32),
        scratch_shapes=[pltpu.VMEM((d, d), BF16)],
        compiler_params=_cparams(("arbitrary", "arbitrary")),
        name="out_projection",
    )(a, z_src, x, mod, w, ln_g, ln_b)


def _col_layout():
    q_cols = np.arange(GRID_W).reshape(N_COL_BLOCKS, Q_BLOCK_W)
    q_start = np.clip(q_cols - KW // 2, 0, GRID_W - KW)
    blk_start = np.clip(np.arange(N_COL_BLOCKS) * Q_BLOCK_W - KW // 2, 0, GRID_W - K_BLOCK_W)
    k_cols = blk_start[:, None] + np.arange(K_BLOCK_W)
    kc = k_cols[:, None, :]
    in_win = (kc >= q_start[:, :, None]) & (kc < q_start[:, :, None] + KW)
    dcol = np.clip(kc - q_cols[:, :, None] + KW - 1, 0, RPB_COLS - 1)
    return blk_start, in_win, dcol


def _window_base(rb, rows):
    return np.clip(rb * Q_ROWS - KH // 2, 0, rows - K_ROWS)


def _na_mask_table(rows):
    _, in_win, _ = _col_layout()
    n_rb = rows // Q_ROWS
    out = np.zeros((3, N_COL_BLOCKS, Q_ROWS, K_ROWS // ROWS_PER_LANE_TILE, Q_BLOCK_W, LANES), np.float32)
    for cls, rb in enumerate((0, 1, n_rb - 1)):
        kr0 = _window_base(rb, rows)
        for qr in range(Q_ROWS):
            r = rb * Q_ROWS + qr
            r0 = np.clip(r - KH // 2, 0, rows - KH)
            for kr in range(K_ROWS):
                row_ok = r0 <= kr0 + kr < r0 + KH
                g, j = divmod(kr, ROWS_PER_LANE_TILE)
                ok = in_win & row_ok
                out[cls, :, qr, g, :, j * K_BLOCK_W:(j + 1) * K_BLOCK_W] = np.where(ok, 0.0, NEG_INF)
    return out


def _rpb_expand_kernel(r_ref, e_ref, o_ref):
    r = r_ref[...]
    hi = r.astype(BF16)
    r1 = r - hi.astype(F32)
    mid = r1.astype(BF16)
    lo = (r1 - mid.astype(F32)).astype(BF16)
    e = e_ref[0]
    o_ref[0] = (_dot(hi, e) + _dot(mid, e) + _dot(lo, e)) * LOG2E


def _na_bias_table(rpb):
    h = rpb.shape[0]
    _, _, dcol = _col_layout()
    kpad = 32
    nj = ROWS_PER_LANE_TILE
    sel = dcol[:, None, :, :] == np.arange(kpad)[None, :, None, None]
    onehot = np.zeros((N_COL_BLOCKS, nj, kpad, Q_BLOCK_W, nj, K_BLOCK_W), np.float32)
    for j in range(nj):
        onehot[:, j, :, :, j, :] = sel
    onehot = onehot.reshape(N_COL_BLOCKS, nj * kpad, Q_BLOCK_W * LANES)
    hi_pad = N_DSTART + nj - 1 - DSTART_SHIFT - RPB_ROWS
    rp = jnp.pad(rpb, ((0, 0), (DSTART_SHIFT, hi_pad), (0, kpad - RPB_COLS)))
    shifted = jnp.concatenate([rp[:, j:j + N_DSTART] for j in range(nj)], axis=-1).reshape(h * N_DSTART, nj * kpad)
    t = pl.pallas_call(
        _rpb_expand_kernel,
        grid=(N_COL_BLOCKS,),
        in_specs=[
            pl.BlockSpec(shifted.shape, lambda n: (0, 0)),
            pl.BlockSpec((1, nj * kpad, Q_BLOCK_W * LANES), lambda n: (n, 0, 0)),
        ],
        out_specs=pl.BlockSpec((1, h * N_DSTART, Q_BLOCK_W * LANES), lambda n: (n, 0, 0)),
        out_shape=jax.ShapeDtypeStruct((N_COL_BLOCKS, h * N_DSTART, Q_BLOCK_W * LANES), F32),
        compiler_params=_cparams(("arbitrary",)),
        name="rpb_expand",
    )(shifted, jnp.asarray(onehot, BF16))
    return t.reshape(N_COL_BLOCKS, h // HEADS_PER_TILE, HEADS_PER_TILE, N_DSTART, Q_BLOCK_W, LANES)


def _na_kernel(rows, blocks_per_step, blk_start, q_ref, k_ref, v_ref, kc_ref, vc_ref, t_ref, m_ref, o_ref):
    n_rb = rows // Q_ROWS
    lane = lax.broadcasted_iota(jnp.int32, (1, LANES), 1)
    kctx = kc_ref[0].astype(BF16)
    vctx = vc_ref[0].astype(BF16)
    n_groups = K_ROWS // ROWS_PER_LANE_TILE
    nq = Q_ROWS * Q_BLOCK_W
    rows_n = HEADS_PER_TILE * nq

    def prepare(sub):
        rb = pl.program_id(2) * blocks_per_step + sub
        kr0 = jnp.clip(rb * Q_ROWS - KH // 2, 0, rows - K_ROWS)
        cls = jnp.where(rb == 0, 0, jnp.where(rb == n_rb - 1, 2, 1))
        off = kr0 - rb * Q_ROWS + KH - 1 + DSTART_SHIFT
        q0 = sub * Q_ROWS * GRID_W
        q2s = []
        for n in range(N_COL_BLOCKS):
            qn = jnp.concatenate(
                [q_ref[0, q0 + qr * GRID_W + n * Q_BLOCK_W: q0 + qr * GRID_W + (n + 1) * Q_BLOCK_W, :]
                 for qr in range(Q_ROWS)], axis=0) * (HEAD_DIM ** -0.5 * LOG2E)
            q2s.append(jnp.concatenate(
                [jnp.where((lane >= h * HEAD_DIM) & (lane < (h + 1) * HEAD_DIM), qn, 0.0)
                 for h in range(HEADS_PER_TILE)], axis=0).astype(BF16))
        s_ctx_all = _dot_nt(jnp.concatenate(q2s, axis=0), kctx)
        return kr0, cls, off, q0, q2s, s_ctx_all

    def scores(prep, n):
        kr0, cls, off, _, q2s, s_ctx_all = prep
        c0 = int(blk_start[n])
        window = lambda ref: jnp.concatenate(
            [ref[0, pl.ds(pl.multiple_of((kr0 + kr) * GRID_W + c0, SUBLANES), K_BLOCK_W), :] for kr in range(K_ROWS)],
            axis=0).astype(BF16)
        bias = jnp.concatenate([
            jnp.concatenate([t_ref[n, 0, h, off + ROWS_PER_LANE_TILE * g - qr] + m_ref[cls, n, qr, g]
                             for g in range(n_groups)], axis=1)
            for h in range(HEADS_PER_TILE) for qr in range(Q_ROWS)], axis=0)
        return _dot_nt(q2s[n], window(k_ref)) + bias, s_ctx_all[n * rows_n:(n + 1) * rows_n], window(v_ref)

    def finish(q0, n, s_lat, s_ctx, vn):
        m = jnp.maximum(jnp.max(s_lat, axis=-1, keepdims=True), jnp.max(s_ctx, axis=-1, keepdims=True))
        e_lat = jnp.exp2(s_lat - m)
        e_ctx = jnp.exp2(s_ctx - m)
        den = jnp.sum(e_lat, axis=-1, keepdims=True) + jnp.sum(e_ctx, axis=-1, keepdims=True)
        o2 = (_dot(e_lat.astype(BF16), vn) + _dot(e_ctx.astype(BF16), vctx)) / den
        o_n = o2[0:nq]
        for h in range(1, HEADS_PER_TILE):
            o_n = jnp.where((lane >= h * HEAD_DIM) & (lane < (h + 1) * HEAD_DIM), o2[h * nq:(h + 1) * nq], o_n)
        for qr in range(Q_ROWS):
            o_ref[0, q0 + qr * GRID_W + n * Q_BLOCK_W: q0 + qr * GRID_W + (n + 1) * Q_BLOCK_W, :] = (
                o_n[qr * Q_BLOCK_W:(qr + 1) * Q_BLOCK_W].astype(o_ref.dtype))

    order = [(sub, n) for sub in range(blocks_per_step) for n in range(N_COL_BLOCKS)]
    preps = {0: prepare(0)}
    cur = scores(preps[0], 0)
    for idx, (sub, n) in enumerate(order):
        nxt = None
        if idx + 1 < len(order):
            nsub, nn = order[idx + 1]
            if nsub not in preps:
                preps[nsub] = prepare(nsub)
            nxt = scores(preps[nsub], nn)
        finish(preps[sub][3], n, *cur)
        cur = nxt


def _neighbourhood_attention(proj, proj_c, bias_table, mask_table):
    b, s, d3 = proj.shape
    d = d3 // 3
    c = proj_c.shape[1]
    rows = s // GRID_W
    n_tiles = d // LANES
    blk_start, _, _ = _col_layout()
    blocks_per_step = min(NA_BLOCKS_PER_STEP, rows // Q_ROWS)
    tq = blocks_per_step * Q_ROWS * GRID_W
    return pl.pallas_call(
        functools.partial(_na_kernel, rows, blocks_per_step, blk_start),
        grid=(b, n_tiles, s // tq),
        in_specs=[
            pl.BlockSpec((1, tq, LANES), lambda bi, hp, rb: (bi, rb, hp)),
            pl.BlockSpec((1, s, LANES), lambda bi, hp, rb: (bi, 0, n_tiles + hp)),
            pl.BlockSpec((1, s, LANES), lambda bi, hp, rb: (bi, 0, 2 * n_tiles + hp)),
            pl.BlockSpec((1, c, LANES), lambda bi, hp, rb: (bi, 0, n_tiles + hp)),
            pl.BlockSpec((1, c, LANES), lambda bi, hp, rb: (bi, 0, 2 * n_tiles + hp)),
            pl.BlockSpec(bias_table.shape[:1] + (1,) + bias_table.shape[2:], lambda bi, hp, rb: (0, hp, 0, 0, 0, 0)),
            pl.BlockSpec(mask_table.shape, lambda bi, hp, rb: (0, 0, 0, 0, 0, 0)),
        ],
        out_specs=pl.BlockSpec((1, tq, LANES), lambda bi, hp, rb: (bi, rb, hp)),
        out_shape=jax.ShapeDtypeStruct((b, s, d), BF16),
        compiler_params=_cparams(("arbitrary", "arbitrary", "arbitrary")),
        name="neighbourhood_attention",
    )(proj, proj, proj, proj_c, proj_c, bias_table, mask_table)


def _ctx_attn_kernel(q_ref, k_ref, v_ref, o_ref):
    lane = lax.broadcasted_iota(jnp.int32, (1, LANES), 1)
    for t in range(q_ref.shape[-1] // LANES):
        cols = slice(t * LANES, (t + 1) * LANES)
        q = q_ref[0, :, cols] * (HEAD_DIM ** -0.5 * LOG2E)
        k = k_ref[0, :, cols].astype(BF16)
        v = v_ref[0, :, cols].astype(BF16)
        out = None
        for h in range(HEADS_PER_TILE):
            in_head = (lane >= h * HEAD_DIM) & (lane < (h + 1) * HEAD_DIM)
            s = _dot_nt(jnp.where(in_head, q, 0.0).astype(BF16), k)
            e = jnp.exp2(s - jnp.max(s, axis=-1, keepdims=True))
            o_h = _dot(e.astype(BF16), v) / jnp.sum(e, axis=-1, keepdims=True)
            out = o_h if out is None else jnp.where(in_head, o_h, out)
        o_ref[0, :, cols] = out.astype(o_ref.dtype)


def _context_attention(proj_c):
    b, c, d3 = proj_c.shape
    d = d3 // 3
    return pl.pallas_call(
        _ctx_attn_kernel,
        grid=(b,),
        in_specs=[
            pl.BlockSpec((1, c, d), lambda bi: (bi, 0, 0)),
            pl.BlockSpec((1, c, d), lambda bi: (bi, 0, 1)),
            pl.BlockSpec((1, c, d), lambda bi: (bi, 0, 2)),
        ],
        out_specs=pl.BlockSpec((1, c, d), lambda bi: (bi, 0, 0)),
        out_shape=jax.ShapeDtypeStruct((b, c, d), BF16),
        compiler_params=_cparams(("arbitrary",)),
        name="context_attention",
    )(proj_c, proj_c, proj_c)


def _conv_block(length):
    return min(CONV_BLOCK, length)


def _dft_matrices(p):
    k = np.arange(p, dtype=np.float64)[:, None]
    n = np.arange(p, dtype=np.float64)[None, :]
    ang = 2.0 * np.pi * k * n / (2 * p)
    f_im = -np.sin(ang)
    f_im[0] = np.cos(np.pi * n[0])
    fwd = np.concatenate([np.cos(ang), f_im], axis=0)
    wk = np.where(k == 0, 1.0, 2.0) / (2 * p)
    ang_i = 2.0 * np.pi * k * (n + p) / (2 * p)
    g_im = -np.sin(ang_i) * wk
    g_im[0] = np.cos(np.pi * (n[0] + p)) / (2 * p)
    inv = np.concatenate([np.cos(ang_i) * wk, g_im], axis=0).T
    return fwd.astype(np.float32), inv.astype(np.float32)


def _filter_kernel(length, p, z_ref, w1_ref, b1_ref, w2_ref, b2_ref, w3_ref, b3_ref, fr_ref, w4_ref,
                   dl_ref, f_ref, o_ref, a_scr, t_scr, prev_scr, norm_scr):
    nb = length // p
    di = pl.program_id(2)
    cb = o_ref.shape[-1]

    @pl.when((pl.program_id(0) == 0) & (pl.program_id(1) == 0) & (di == 0))
    def _():
        a = jnp.sin(fr_ref[0:1, :] * (_dot3(z_ref[...], w1_ref[...]) + b1_ref[...]))
        a = jnp.sin(fr_ref[1:2, :] * (_dot3(a, w2_ref[...]) + b2_ref[...]))
        a_scr[...] = jnp.sin(fr_ref[2:3, :] * (_dot3(a, w3_ref[...]) + b3_ref[...]))

    def block_dft(blk):
        start = pl.multiple_of(blk * p, p)
        return _dot(f_ref[...], t_scr[pl.ds(start, p), :].astype(BF16))

    @pl.when(di == 0)
    def _():
        total = jnp.zeros((1, cb), F32)
        for blk in range(2 * nb):
            back = blk < nb
            idx = lax.broadcasted_iota(jnp.int32, (p, cb), 0) + (blk % nb) * p
            t = ((length - idx) if back else idx).astype(F32) * (1.0 / (length - 1))
            rows_a = (blk % nb) * p
            taps = (_dot(a_scr[rows_a:rows_a + p, :].astype(BF16), w4_ref[0, 0 if back else 1].astype(BF16))
                    * jnp.exp(-t * dl_ref[...]))
            if blk == 0:
                taps = jnp.where(idx == 0, 0.0, taps)
            t_scr[blk * p:(blk + 1) * p, :] = taps
            total = total + jnp.sum(jnp.abs(taps), axis=0, keepdims=True)
        norm_scr[...] = jnp.broadcast_to(1.0 / total, norm_scr.shape)
        prev_scr[...] = block_dft(0)

    nxt = block_dft(di + 1)
    krow = lax.broadcasted_iota(jnp.int32, (2 * p, cb), 0)
    o_ref[0, 0] = ((prev_scr[...] + jnp.where((krow & 1) == 0, nxt, -nxt)) * norm_scr[0:1, :]).astype(o_ref.dtype)
    prev_scr[...] = nxt


def _position_features(length, hid):
    t = np.linspace(0.0, 1.0, length)[:, None]
    bands = (HY_EMB_DIM - 1) // 2
    wpos = 2.0 * np.pi * np.arange(length)[:, None] / length
    f = np.linspace(1e-4, bands - 1, bands)[None, :]
    z = np.concatenate([t, np.cos(f * wpos), -np.sin(f * wpos)], axis=-1)
    out = np.zeros((length, 2 * hid), np.float32)
    out[1:, :HY_EMB_DIM] = z[1:][::-1]
    out[:, hid:hid + HY_EMB_DIM] = z
    return out


def _hyena_filters(length, w1, b1, w2, b2, w3, b3, w4, freq):
    p = _conv_block(length)
    nb = length // p
    width = w4.shape[-1] // 4
    cb = min(FILTER_LANES, width)
    hid = w2.shape[0]
    z_all = jnp.asarray(_position_features(length, hid))
    both = lambda w: jnp.kron(jnp.eye(2, dtype=F32), w)
    twice = lambda v: jnp.tile(v.reshape(-1, hid), (1, 2))
    w1p = both(jnp.pad(w1, ((0, hid - HY_EMB_DIM), (0, 0))))
    w2, w3 = both(w2), both(w3)
    b1, b2, b3, freq = twice(b1), twice(b2), twice(b3), twice(freq)
    order = w4.shape[-1] // (2 * width)
    w4r = jnp.transpose(w4.reshape(w4.shape[0], order, 2, width), (1, 2, 0, 3))
    zero = jnp.zeros_like(w4r[:, 0])
    w4r = jnp.stack([jnp.concatenate([w4r[:, 1], zero], axis=1),
                     jnp.concatenate([zero, w4r[:, 0]], axis=1)], axis=1)
    hid = 2 * hid
    max_decay = math.log(HY_TARGET) / HY_FAST_DECAY
    min_decay = math.log(HY_TARGET) / HY_SLOW_DECAY
    deltas = jnp.abs(jnp.linspace(min_decay, max_decay, width, dtype=F32)).reshape(1, width)
    fwd = jnp.asarray(_dft_matrices(p)[0]).astype(BF16)
    full = lambda shape: pl.BlockSpec(shape, lambda o, c, di: (0,) * len(shape))
    return pl.pallas_call(
        functools.partial(_filter_kernel, length, p),
        grid=(order, width // cb, 2 * nb - 1),
        in_specs=[
            full(z_all.shape), full(w1p.shape), full((1, hid)), full(w2.shape), full((1, hid)),
            full(w3.shape), full((1, hid)), full(freq.shape),
            pl.BlockSpec((1, 2, hid, cb), lambda o, c, di: (o, 0, 0, c)),
            pl.BlockSpec((1, cb), lambda o, c, di: (0, c)),
            full(fwd.shape),
        ],
        out_specs=pl.BlockSpec((1, 1, 2 * p, cb), lambda o, c, di: (o, di, 0, c)),
        out_shape=jax.ShapeDtypeStruct((order, 2 * nb - 1, 2 * p, width), BF16),
        scratch_shapes=[pltpu.VMEM((length, hid), F32), pltpu.VMEM((2 * length, cb), F32),
                        pltpu.VMEM((2 * p, cb), F32), pltpu.VMEM((SUBLANES, cb), F32)],
        compiler_params=_cparams(("arbitrary", "arbitrary", "arbitrary")),
        name="hyena_filters",
    )(z_all, w1p, b1, w2, b2, w3, b3, freq, w4r, deltas, fwd)


def _short_conv_block(pad_ref, j, p, w_ref, b_ref):
    base = SUBLANES + j * p
    return (pad_ref[base - 1: base - 1 + p, :] * w_ref[0:1, :] + pad_ref[base: base + p, :] * w_ref[1:2, :]
            + pad_ref[base + 1: base + 1 + p, :] * w_ref[2:3, :] + b_ref[...])


def _zero_halo(pad_ref, length):
    zeros = jnp.zeros((SUBLANES, pad_ref.shape[1]), F32)
    pad_ref[0:SUBLANES, :] = zeros
    pad_ref[SUBLANES + length: 2 * SUBLANES + length, :] = zeros


def _fill_rows(pad_ref, src_ref, start, stop):
    pad_ref[SUBLANES + start: SUBLANES + stop, :] = src_ref[start:stop, :].astype(F32)


def _conv_kernel(length, p, conv_u, u_ref, m_ref, h_ref, f_ref, g_ref, cwu_ref, cbu_ref, cwm_ref, cbm_ref,
                 skip_ref, o_ref, pad, uconv, uhat, yhat):
    nb = length // p
    nbat = u_ref.shape[0]
    lanes = lambda bi: slice(bi * LANES, (bi + 1) * LANES)

    for bi in range(nbat):
        _zero_halo(pad.at[bi], length)
        if conv_u:
            _fill_rows(pad.at[bi], u_ref.at[bi], 0, p)
    fwd = f_ref[...]
    for j in range(nb):
        blocks = []
        for bi in range(nbat):
            if conv_u:
                if j + 1 < nb:
                    _fill_rows(pad.at[bi], u_ref.at[bi], (j + 1) * p, (j + 2) * p)
                uj = _short_conv_block(pad.at[bi], j, p, cwu_ref, cbu_ref)
                uconv[bi, j * p:(j + 1) * p, :] = uj
            else:
                uj = u_ref[bi, j * p:(j + 1) * p, :]
            blocks.append(uj.astype(BF16))
        uhat[j] = _dot(fwd, jnp.concatenate(blocks, axis=1))
    for bi in range(nbat):
        _fill_rows(pad.at[bi], m_ref.at[bi], 0, p)

    inv = g_ref[...]
    first = lax.broadcasted_iota(jnp.int32, (SUBLANES, nbat * LANES), 0) == 0
    for i in range(nb):
        for c in range(p // ACC_ROWS):
            ra = slice(c * ACC_ROWS, (c + 1) * ACC_ROWS)
            rb = slice(p + c * ACC_ROWS, p + (c + 1) * ACC_ROWS)
            acc = [[jnp.zeros((ACC_ROWS, LANES), F32) for _ in range(2)] for _ in range(nbat)]
            for j in range(nb):
                h_d = h_ref.at[0, i - j + nb - 1]
                ha, hb = h_d[ra, :].astype(F32), h_d[rb, :].astype(F32)
                for bi in range(nbat):
                    ua, ub = uhat[j, ra, lanes(bi)], uhat[j, rb, lanes(bi)]
                    acc[bi][0] = acc[bi][0] + (ha * ua - hb * ub)
                    acc[bi][1] = acc[bi][1] + (ha * ub + hb * ua)
            for bi in range(nbat):
                yhat[ra, lanes(bi)] = acc[bi][0]
                yhat[rb, lanes(bi)] = acc[bi][1]
        dc = jnp.zeros((SUBLANES, nbat * LANES), F32)
        ny = jnp.zeros((SUBLANES, nbat * LANES), F32)
        for j in range(nb):
            h_d = h_ref.at[0, i - j + nb - 1]
            h_dc = h_d[0:2 * SUBLANES, :].astype(F32)[0:SUBLANES]
            h_ny = h_d[p:p + 2 * SUBLANES, :].astype(F32)[0:SUBLANES]
            dc = dc + jnp.concatenate([h_dc] * nbat, axis=1) * uhat[j, 0:SUBLANES, :]
            ny = ny + jnp.concatenate([h_ny] * nbat, axis=1) * uhat[j, p:p + SUBLANES, :]
        yhat[0:SUBLANES, :] = jnp.where(first, dc, yhat[0:SUBLANES, :])
        yhat[p:p + SUBLANES, :] = jnp.where(first, ny, yhat[p:p + SUBLANES, :])

        y = _dot(inv, yhat[...].astype(BF16))
        for bi in range(nbat):
            if i + 1 < nb:
                _fill_rows(pad.at[bi], m_ref.at[bi], (i + 1) * p, (i + 2) * p)
            mi = _short_conv_block(pad.at[bi], i, p, cwm_ref, cbm_ref)
            src = uconv if conv_u else u_ref
            yi = y[:, lanes(bi)] + src[bi, i * p:(i + 1) * p, :].astype(F32) * skip_ref[...]
            o_ref[bi, i * p:(i + 1) * p, :] = (mi * yi).astype(o_ref.dtype)


def _long_conv_gate(u_src, u_col, conv_u, proj, m_col, spectra, order, conv_w, conv_b, skip, out_dtype):
    b, length, _ = proj.shape
    width = skip.shape[-1]
    p = _conv_block(length)
    nb = length // p
    cb = LANES
    ncb = width // cb
    nbat = CONV_BATCH if nb > 1 else b
    assert b % nbat == 0 and p % ACC_ROWS == 0
    fwd_np, inv_np = _dft_matrices(p)
    fwd = jnp.asarray(fwd_np).astype(BF16)
    inv = jnp.asarray(inv_np).astype(BF16)
    col = lambda base: (lambda c, bi: (bi, 0, base * ncb + c))
    par = lambda base: (lambda c, bi: (0, base * ncb + c))
    once = pl.Buffered(1)
    return pl.pallas_call(
        functools.partial(_conv_kernel, length, p, conv_u),
        grid=(ncb, b // nbat),
        in_specs=[
            pl.BlockSpec((nbat, length, cb), col(u_col)),
            pl.BlockSpec((nbat, length, cb), col(m_col)),
            pl.BlockSpec((1, 2 * nb - 1, 2 * p, cb), lambda c, bi: (order, 0, 0, c)),
            pl.BlockSpec(fwd.shape, lambda c, bi: (0, 0), pipeline_mode=once),
            pl.BlockSpec(inv.shape, lambda c, bi: (0, 0), pipeline_mode=once),
            pl.BlockSpec((3, cb), par(0)),
            pl.BlockSpec((1, cb), par(0)),
            pl.BlockSpec((3, cb), par(m_col)),
            pl.BlockSpec((1, cb), par(m_col)),
            pl.BlockSpec((1, cb), lambda c, bi: (0, c)),
        ],
        out_specs=pl.BlockSpec((nbat, length, cb), lambda c, bi: (bi, 0, c)),
        out_shape=jax.ShapeDtypeStruct((b, length, width), out_dtype),
        scratch_shapes=[
            pltpu.VMEM((nbat, length + 2 * SUBLANES, cb), F32),
            pltpu.VMEM((nbat, length, cb) if conv_u else (nbat, SUBLANES, cb), F32),
            pltpu.VMEM((nb, 2 * p, nbat * cb), F32), pltpu.VMEM((2 * p, nbat * cb), F32),
        ],
        compiler_params=_cparams(("arbitrary", "arbitrary")),
        name="long_conv_gate",
    )(u_src, proj, spectra, fwd, inv, conv_w, conv_b.reshape(1, -1), conv_w, conv_b.reshape(1, -1),
      skip[order].reshape(1, width))


def _hyena_core(proj, spectra, conv_w, conv_b, skip):
    z = _long_conv_gate(proj, 0, True, proj, 1, spectra, 0, conv_w, conv_b, skip, F32)
    return _long_conv_gate(z, 0, False, proj, 2, spectra, 1, conv_w, conv_b, skip, BF16)


def kernel(x, c, ctx, c_ctx, w_ada, b_ada, w_in, w_out, ln_g, ln_b, na_rpb, hy_conv_w, hy_conv_b, hy_f_w1, hy_f_b1,
           hy_f_w2, hy_f_b2, hy_f_w3, hy_f_b3, hy_f_w4, hy_f_freq, hy_skip):
    depth = w_in.shape[0]
    b, s, d = x.shape
    n_ctx = ctx.shape[1]
    n_mixers = 2
    alpha = (2 * depth) ** 0.25

    cc = jnp.concatenate([c, c_ctx[None, :], jnp.zeros((2 * SUBLANES - b - 1, d), F32)], axis=0)
    mod = _ada_vectors(cc, w_ada, b_ada)
    mod = mod.reshape(depth, mod.shape[1], 1, 3 * d)
    ln_g = ln_g.reshape(depth, 1, d)
    ln_b = ln_b.reshape(depth, 1, d)
    mask_table = jnp.asarray(_na_mask_table(s // GRID_W))

    for i in range(depth):
        last = i == depth - 1
        j = i // n_mixers
        attn = i % n_mixers == 0
        proj_dtype = F32 if attn else BF16
        proj, z = _in_projection(x, mod, i, None, w_in, proj_dtype, split_gate=attn)
        proj_c = z_c = None
        if attn or not last:
            proj_c, z_c = _in_projection(ctx, mod, i, b, w_in, proj_dtype, split_gate=attn)
        if attn:
            bias_table = _na_bias_table(na_rpb[j])
            a = _neighbourhood_attention(proj, proj_c, bias_table, mask_table)
            a_c = None if last else _context_attention(proj_c)
        else:
            filt = (hy_f_w1[j], hy_f_b1[j], hy_f_w2[j], hy_f_b2[j], hy_f_w3[j], hy_f_b3[j], hy_f_w4[j], hy_f_freq[j])
            a = _hyena_core(proj, _hyena_filters(s, *filt), hy_conv_w[j], hy_conv_b[j], hy_skip[j])
            a_c = None if last else _hyena_core(proj_c, _hyena_filters(n_ctx, *filt), hy_conv_w[j], hy_conv_b[j],
                                                hy_skip[j])
        x = _out_projection(a, z if attn else proj, x, mod, i, None, w_out, ln_g, ln_b, alpha)
        if not last:
            ctx = _out_projection(a_c, z_c if attn else proj_c, ctx, mod, i, b, w_out, ln_g, ln_b, alpha)
    return x
```
